```python
import math
import jax
import jax.numpy as jnp
from jax import lax
import numpy as np

D_MODEL = 2048
BATCH = 2
SEQ = 4096
DEPTH = 4

N_A_LAYERS = DEPTH // 2
N_B_LAYERS = DEPTH - N_A_LAYERS

SSM_WIDTH = D_MODEL // 2
SSM_GROUP = 16
SSM_GROUPS = SSM_WIDTH // SSM_GROUP
SSM_STATE = 64
DT_MIN = 0.001
DT_MAX = 0.1

N_HEADS = 16
HEAD_DIM = 128
N_KV_HEADS = 4
HEADS_PER_KV = N_HEADS // N_KV_HEADS
ATTN_WIDTH = N_HEADS * HEAD_DIM
KV_WIDTH = N_KV_HEADS * HEAD_DIM
N_BRANCH = 3
CMP_BLOCK = 32
CMP_STRIDE = 16
SEL_BLOCK = 64
N_SELECT = 16
WINDOW = 512
Q_BLOCK = 64
B_IN_WIDTH = ATTN_WIDTH * (1 + N_BRANCH) + N_BRANCH * N_HEADS

LN_EPS = 1e-5
DEEPNORM_ALPHA = (2 * DEPTH) ** 0.25
DEEPNORM_BETA = (8 * DEPTH) ** -0.25
MASK_VALUE = -1e30
FORCE_SCORE = 1e6

kernel_name = "yoco_s5_nsa_deepnorm_trunk"


def layer_norm(x, g, b):
    x32 = x.astype(jnp.float32)
    mu = jnp.mean(x32, axis=-1, keepdims=True)
    var = jnp.mean(jnp.square(x32 - mu), axis=-1, keepdims=True)
    return ((x32 - mu) * lax.rsqrt(var + LN_EPS) * g.astype(jnp.float32) + b.astype(jnp.float32)).astype(x.dtype)


def _complex_linear_combine(e1, e2):
    a1r, a1i, b1r, b1i = e1
    a2r, a2i, b2r, b2i = e2
    return (a2r * a1r - a2i * a1i,
            a2r * a1i + a2i * a1r,
            a2r * b1r - a2i * b1i + b2r,
            a2r * b1i + a2i * b1r + b2i)


def s5_mixer(x, w_in, lam_re, lam_im, log_dt, b_re, b_im, c_re, c_im, d_skip, w_glu, b_glu, w_out):
    bsz, seq, _ = x.shape
    f32 = jnp.float32
    u, z = jnp.split(x @ w_in, 2, axis=-1)
    lr = lam_re.astype(f32)
    li = lam_im.astype(f32)
    dt = jnp.exp(log_dt.astype(f32))[:, None]
    mag = jnp.exp(lr * dt)
    ar = mag * jnp.cos(li * dt)
    ai = mag * jnp.sin(li * dt)
    inv_abs2 = 1.0 / (lr * lr + li * li)
    cr = ((ar - 1.0) * lr + ai * li) * inv_abs2
    ci = (ai * lr - (ar - 1.0) * li) * inv_abs2
    br = b_re.astype(f32)
    bi = b_im.astype(f32)
    bbar_r = cr[..., None] * br - ci[..., None] * bi
    bbar_i = cr[..., None] * bi + ci[..., None] * br
    ug = jnp.swapaxes(u, 0, 1).astype(f32).reshape(seq, bsz, SSM_GROUPS, SSM_GROUP)
    bu_r = jnp.einsum('lbgc,gpc->lbgp', ug, bbar_r)
    bu_i = jnp.einsum('lbgc,gpc->lbgp', ug, bbar_i)
    a_shape = (seq, 1, SSM_GROUPS, SSM_STATE)
    _, _, h_r, h_i = lax.associative_scan(
        _complex_linear_combine,
        (jnp.broadcast_to(ar, a_shape), jnp.broadcast_to(ai, a_shape), bu_r, bu_i),
        axis=0)
    y = (jnp.einsum('gcp,lbgp->lbgc', c_re.astype(f32), h_r)
         - jnp.einsum('gcp,lbgp->lbgc', c_im.astype(f32), h_i))
    y = jnp.swapaxes(y.reshape(seq, bsz, SSM_WIDTH), 0, 1).astype(x.dtype) + d_skip * u
    g = jax.nn.gelu(y)
    y = g * jax.nn.sigmoid(g @ w_glu + b_glu)
    return (y * jax.nn.silu(z)) @ w_out


def shared_kv(h, kv_w, pos_k, w1_k, w2_k, pos_v, w1_v, w2_v):
    bsz, seq, _ = h.shape
    kv = (h @ kv_w).reshape(bsz, seq, 2 * N_BRANCH, N_KV_HEADS, HEAD_DIM)
    k_c, v_c, k_s, v_s, k_w, v_w = [kv[:, :, i] for i in range(2 * N_BRANCH)]
    n_cmp = (seq - CMP_BLOCK) // CMP_STRIDE + 1
    idx = jnp.arange(n_cmp)[:, None] * CMP_STRIDE + jnp.arange(CMP_BLOCK)[None, :]

    def compress(t, pos, w1, w2):
        blk = t[:, idx] + pos[None, None, :, None, :]
        blk = jnp.moveaxis(blk, 3, 2).reshape(bsz, n_cmp, N_KV_HEADS, CMP_BLOCK * HEAD_DIM)
        return jax.nn.gelu(blk @ w1) @ w2

    return (compress(k_c, pos_k, w1_k, w2_k), compress(v_c, pos_v, w1_v, w2_v), k_s, v_s, k_w, v_w)


def masked_softmax(s, valid):
    s = jnp.where(valid, s, MASK_VALUE)
    m = jnp.max(s, axis=-1, keepdims=True)
    e = jnp.where(valid, jnp.exp(s - m), 0.0)
    return e / jnp.maximum(jnp.sum(e, axis=-1, keepdims=True), 1e-30)


def nsa_attention(q, k_cmp, v_cmp, k_slc, v_slc, k_win, v_win):
    bsz, seq = q.shape[:2]
    dtype = q.dtype
    f32 = jnp.float32
    n_q = seq // Q_BLOCK
    n_cmp = k_cmp.shape[1]
    n_sblk = seq // SEL_BLOCK
    n_sel = min(N_SELECT, n_sblk)
    scale = HEAD_DIM ** -0.5
    cmp_start = jnp.arange(n_cmp) * CMP_STRIDE
    cmp_end = cmp_start + CMP_BLOCK - 1
    sel_start = jnp.arange(n_sblk) * SEL_BLOCK
    overlap = ((cmp_start[:, None] < sel_start[None, :] + SEL_BLOCK)
               & (cmp_end[:, None] >= sel_start[None, :])).astype(f32)
    kb_s = k_slc.reshape(bsz, n_sblk, SEL_BLOCK, N_KV_HEADS, HEAD_DIM).transpose(0, 3, 1, 2, 4)
    vb_s = v_slc.reshape(bsz, n_sblk, SEL_BLOCK, N_KV_HEADS, HEAD_DIM).transpose(0, 3, 1, 2, 4)
    k_w_pad = jnp.pad(k_win, ((0, 0), (WINDOW, 0), (0, 0), (0, 0)))
    v_w_pad = jnp.pad(v_win, ((0, 0), (WINDOW, 0), (0, 0), (0, 0)))
    qg = q.reshape(bsz, n_q, Q_BLOCK, N_KV_HEADS, HEADS_PER_KV, HEAD_DIM).transpose(1, 0, 2, 3, 4, 5)
    gather_blocks = jax.vmap(jax.vmap(lambda blocks, ids: blocks[ids]))
    blk_ids = jnp.arange(n_sblk)

    def block_fn(args):
        i, qb = args
        t = i * Q_BLOCK + jnp.arange(Q_BLOCK)
        qb = qb * scale
        s = jnp.einsum('bqghd,bngd->bghqn', qb, k_cmp).astype(f32)
        p_cmp = masked_softmax(s, cmp_end[None, :] <= t[:, None])
        o_cmp = jnp.einsum('bghqn,bngd->bqghd', p_cmp.astype(dtype), v_cmp)
        imp = jnp.einsum('bghqn,nj->bgqj', p_cmp, overlap)
        cur = t // SEL_BLOCK
        future = blk_ids[None, :] > cur[:, None]
        forced = ((blk_ids[None, :] == 0) | (blk_ids[None, :] == cur[:, None])
                  | (blk_ids[None, :] == cur[:, None] - 1))
        imp = jnp.where(future, MASK_VALUE, jnp.where(forced, FORCE_SCORE, imp))
        _, sel = lax.top_k(imp, n_sel)
        ks = gather_blocks(kb_s, sel).reshape(bsz, N_KV_HEADS, Q_BLOCK, n_sel * SEL_BLOCK, HEAD_DIM)
        vs = gather_blocks(vb_s, sel).reshape(bsz, N_KV_HEADS, Q_BLOCK, n_sel * SEL_BLOCK, HEAD_DIM)
        pos = (sel[..., None] * SEL_BLOCK + jnp.arange(SEL_BLOCK)).reshape(bsz, N_KV_HEADS, Q_BLOCK, n_sel * SEL_BLOCK)
        valid = (pos <= t[None, None, :, None])[:, :, None]
        s = jnp.einsum('bqghd,bgqkd->bghqk', qb, ks).astype(f32)
        p = masked_softmax(s, valid)
        o_slc = jnp.einsum('bghqk,bgqkd->bqghd', p.astype(dtype), vs)
        kw = lax.dynamic_slice_in_dim(k_w_pad, i * Q_BLOCK, WINDOW + Q_BLOCK, axis=1)
        vw = lax.dynamic_slice_in_dim(v_w_pad, i * Q_BLOCK, WINDOW + Q_BLOCK, axis=1)
        kpos = i * Q_BLOCK - WINDOW + jnp.arange(WINDOW + Q_BLOCK)
        valid = ((kpos[None, :] <= t[:, None]) & (kpos[None, :] > t[:, None] - WINDOW)
                 & (kpos[None, :] >= 0))
        s = jnp.einsum('bqghd,bkgd->bghqk', qb, kw).astype(f32)
        p = masked_softmax(s, valid)
        o_win = jnp.einsum('bghqk,bkgd->bqghd', p.astype(dtype), vw)
        return o_cmp, o_slc, o_win

    o_cmp, o_slc, o_win = lax.map(block_fn, (jnp.arange(n_q), qg))

    def unblock(o):
        return o.transpose(1, 0, 2, 3, 4, 5).reshape(bsz, seq, N_HEADS, HEAD_DIM)

    return (unblock(o_cmp), unblock(o_slc), unblock(o_win))


def nsa_mixer(x, w_in, w_out, shared):
    bsz, seq, _ = x.shape
    proj = x @ w_in
    q = proj[..., :ATTN_WIDTH].reshape(bsz, seq, N_HEADS, HEAD_DIM)
    z = proj[..., ATTN_WIDTH:ATTN_WIDTH * (1 + N_BRANCH)].reshape(bsz, seq, N_BRANCH, N_HEADS, HEAD_DIM)
    gate = jax.nn.sigmoid(proj[..., ATTN_WIDTH * (1 + N_BRANCH):].reshape(bsz, seq, N_BRANCH, N_HEADS))
    o = jnp.stack(nsa_attention(q, *shared), axis=2)
    y = jnp.sum(gate[..., None] * o * jax.nn.silu(z), axis=2).reshape(bsz, seq, ATTN_WIDTH)
    return y @ w_out


def setup_inputs(seed: int = 0) -> dict:
    key = jax.random.key(seed)
    ks = jax.random.split(key, 24)
    nrm = jax.random.normal
    f32 = jnp.float32
    na, nb = N_A_LAYERS, N_B_LAYERS
    g, p, c, e = SSM_GROUPS, SSM_STATE, SSM_GROUP, SSM_WIDTH
    lam_im0 = math.pi * jnp.arange(p, dtype=f32)
    return {
        "x": nrm(ks[0], (BATCH, SEQ, D_MODEL), f32),
        "a_w_in": nrm(ks[1], (na, D_MODEL, 2 * e), f32) * D_MODEL ** -0.5,
        "a_lam_re": -0.5 + 0.01 * nrm(ks[2], (na, g, p), f32),
        "a_lam_im": lam_im0 + 0.01 * nrm(ks[3], (na, g, p), f32),
        "a_log_dt": jax.random.uniform(ks[4], (na, g), f32, math.log(DT_MIN), math.log(DT_MAX)),
        "a_b_re": nrm(ks[5], (na, g, p, c), f32) * (2 * c) ** -0.5,
        "a_b_im": nrm(ks[6], (na, g, p, c), f32) * (2 * c) ** -0.5,
        "a_c_re": nrm(ks[7], (na, g, c, p), f32) * p ** -0.5,
        "a_c_im": nrm(ks[8], (na, g, c, p), f32) * p ** -0.5,
        "a_d": nrm(ks[9], (na, e), f32),
        "a_w_glu": nrm(ks[10], (na, e, e), f32) * e ** -0.5,
        "a_b_glu": 0.01 * nrm(ks[11], (na, e), f32),
        "a_w_out": nrm(ks[12], (na, e, D_MODEL), f32) * e ** -0.5 * DEEPNORM_BETA,
        "kv_w": nrm(ks[13], (D_MODEL, 2 * N_BRANCH * KV_WIDTH), f32) * D_MODEL ** -0.5,
        "cmp_pos_k": 0.02 * nrm(ks[14], (CMP_BLOCK, HEAD_DIM), f32),
        "cmp_w1_k": nrm(ks[15], (CMP_BLOCK * HEAD_DIM, HEAD_DIM), f32) * (CMP_BLOCK * HEAD_DIM) ** -0.5,
        "cmp_w2_k": nrm(ks[16], (HEAD_DIM, HEAD_DIM), f32) * HEAD_DIM ** -0.5,
        "cmp_pos_v": 0.02 * nrm(ks[17], (CMP_BLOCK, HEAD_DIM), f32),
        "cmp_w1_v": nrm(ks[18], (CMP_BLOCK * HEAD_DIM, HEAD_DIM), f32) * (CMP_BLOCK * HEAD_DIM) ** -0.5,
        "cmp_w2_v": nrm(ks[19], (HEAD_DIM, HEAD_DIM), f32) * HEAD_DIM ** -0.5,
        "b_w_in": nrm(ks[20], (nb, D_MODEL, B_IN_WIDTH), f32) * D_MODEL ** -0.5,
        "b_w_out": nrm(ks[21], (nb, ATTN_WIDTH, D_MODEL), f32) * ATTN_WIDTH ** -0.5 * DEEPNORM_BETA,
        "ln_g": 1.0 + 0.01 * nrm(ks[22], (DEPTH, D_MODEL), f32),
        "ln_b": 0.01 * nrm(ks[23], (DEPTH, D_MODEL), f32),
    }


def reference(x, a_w_in, a_lam_re, a_lam_im, a_log_dt, a_b_re, a_b_im, a_c_re, a_c_im, a_d,
              a_w_glu, a_b_glu, a_w_out, kv_w, cmp_pos_k, cmp_w1_k, cmp_w2_k, cmp_pos_v,
              cmp_w1_v, cmp_w2_v, b_w_in, b_w_out, ln_g, ln_b):
    shared = None
    for layer in range(DEPTH):
        if layer < N_A_LAYERS:
            i = layer
            y = s5_mixer(x, a_w_in[i], a_lam_re[i], a_lam_im[i], a_log_dt[i], a_b_re[i], a_b_im[i],
                         a_c_re[i], a_c_im[i], a_d[i], a_w_glu[i], a_b_glu[i], a_w_out[i])
        else:
            i = layer - N_A_LAYERS
            y = nsa_mixer(x, b_w_in[i], b_w_out[i], shared)
        x = layer_norm(DEEPNORM_ALPHA * x + y, ln_g[layer], ln_b[layer])
        if layer == N_A_LAYERS - 1:
            shared = shared_kv(x, kv_w, cmp_pos_k, cmp_w1_k, cmp_w2_k, cmp_pos_v, cmp_w1_v, cmp_w2_v)
    return x
```

```python
import functools
import math

import jax
import jax.numpy as jnp
from jax import lax
from jax.experimental import pallas as pl
from jax.experimental.pallas import tpu as pltpu

F32 = jnp.float32
BF16 = jnp.bfloat16

HEAD_DIM = 128
N_HEADS = 16
N_KV_HEADS = 4
HEADS_PER_KV = N_HEADS // N_KV_HEADS
GROUP_WIDTH = HEADS_PER_KV * HEAD_DIM
N_BRANCH = 3
CMP_BLOCK = 32
CMP_STRIDE = 16
SEL_BLOCK = 64
N_SELECT = 16
WINDOW = 512
SSM_GROUP = 16
SSM_STATE = 64
SSM_CHUNK_GROUPS = 16
SSM_CHUNK_IN = SSM_CHUNK_GROUPS * SSM_GROUP
SSM_CHUNK_STATE = SSM_CHUNK_GROUPS * SSM_STATE
LN_EPS = 1e-5
MASK_VALUE = -1e30
FORCE_SCORE = 1e6
PAD_SCORE = -3e38

SUBLANES = 8
LANES = 128
VMEM_LIMIT_BYTES = 48 * 1024 * 1024

SCAN_T = 256
ATT_TQ = 128
ATT_TK = 512


def _params(*sem):
    return pltpu.CompilerParams(dimension_semantics=sem, vmem_limit_bytes=VMEM_LIMIT_BYTES)


def _sigmoid(v):
    return 1.0 / (1.0 + jnp.exp(-v))


def _layer_norm(v, g, b):
    mu = jnp.mean(v, axis=-1, keepdims=True)
    d = v - mu
    var = jnp.mean(d * d, axis=-1, keepdims=True)
    return d * lax.rsqrt(var + LN_EPS) * g + b


def _mm_kernel(a_ref, w_ref, o_ref, *, scale, lane_blocked):
    acc = jnp.dot(a_ref[...], w_ref[...], preferred_element_type=F32)
    if scale is not None:
        acc = acc * scale
    if lane_blocked:
        for j in range(o_ref.shape[0]):
            o_ref[j] = acc[:, j * LANES:(j + 1) * LANES].astype(o_ref.dtype)
    else:
        o_ref[...] = acc.astype(o_ref.dtype)


def _matmul(a, w, out_dtype, *, tm=512, tn=1024, scale=None, lane_blocked=False):
    m, k = a.shape
    n = w.shape[1]
    tm = min(tm, m)
    tn = min(tn, n)
    assert m % tm == 0 and n % tn == 0
    if lane_blocked:
        out_spec = pl.BlockSpec((tn // LANES, tm, LANES), lambda j, i: (j, i, 0))
        out_shape = jax.ShapeDtypeStruct((n // LANES, m, LANES), out_dtype)
    else:
        out_spec = pl.BlockSpec((tm, tn), lambda j, i: (i, j))
        out_shape = jax.ShapeDtypeStruct((m, n), out_dtype)
    return pl.pallas_call(
        functools.partial(_mm_kernel, scale=scale, lane_blocked=lane_blocked),
        grid=(n // tn, m // tm),
        in_specs=[pl.BlockSpec((tm, k), lambda j, i: (i, 0)),
                  pl.BlockSpec((k, tn), lambda j, i: (0, j))],
        out_specs=out_spec,
        out_shape=out_shape,
        compiler_params=_params("parallel", "parallel"),
        name="matmul",
    )(a, w)


def _s5_scan_kernel(u_ref, wb_ref, cb_ref, a_ref, o_ref, bu_ref, hb_ref, carry_ref, *, t_tok):
    s_len = t_tok // SUBLANES
    p = SSM_CHUNK_STATE

    @pl.when(pl.program_id(2) == 0)
    def _():
        carry_ref[...] = jnp.zeros_like(carry_ref)

    n_lb = SSM_CHUNK_IN // LANES
    rows = [jnp.concatenate([u_ref[c, pl.ds(k, SUBLANES, stride=s_len), :] for c in range(n_lb)], axis=1)
            for k in range(s_len)]
    up = jnp.concatenate(rows, axis=0).astype(BF16)
    bu_ref[...] = jnp.dot(up, wb_ref[...], preferred_element_type=F32)

    ar = jnp.broadcast_to(a_ref[0:1, :], (SUBLANES, p))
    ai = jnp.broadcast_to(a_ref[1:2, :], (SUBLANES, p))

    def step(k, hr, hi):
        br = bu_ref[k * SUBLANES:(k + 1) * SUBLANES, 0:p]
        bi = bu_ref[k * SUBLANES:(k + 1) * SUBLANES, p:2 * p]
        return ar * hr - ai * hi + br, ar * hi + ai * hr + bi

    er = jnp.zeros((SUBLANES, p), F32)
    ei = jnp.zeros((SUBLANES, p), F32)
    for k in range(s_len):
        er, ei = step(k, er, ei)

    sr = jnp.broadcast_to(a_ref[2:3, :], (SUBLANES, p))
    si = jnp.broadcast_to(a_ref[3:4, :], (SUBLANES, p))
    sub = lax.broadcasted_iota(jnp.int32, (SUBLANES, p), 0)
    hinr = jnp.where(sub == 0, carry_ref[:, 0:p], 0.0)
    hini = jnp.where(sub == 0, carry_ref[:, p:2 * p], 0.0)
    for j in range(SUBLANES - 1):
        nr = sr * hinr - si * hini + er
        ni = sr * hini + si * hinr + ei
        hinr = jnp.where(sub == j + 1, pltpu.roll(nr, 1, 0), hinr)
        hini = jnp.where(sub == j + 1, pltpu.roll(ni, 1, 0), hini)
    nr = sr * hinr - si * hini + er
    ni = sr * hini + si * hinr + ei
    carry_ref[:, 0:p] = jnp.broadcast_to(nr[SUBLANES - 1:SUBLANES, :], (SUBLANES, p))
    carry_ref[:, p:2 * p] = jnp.broadcast_to(ni[SUBLANES - 1:SUBLANES, :], (SUBLANES, p))

    hr, hi = hinr, hini
    for k in range(s_len):
        hr, hi = step(k, hr, hi)
        hb_ref[k * SUBLANES:(k + 1) * SUBLANES, 0:p] = hr
        hb_ref[k * SUBLANES:(k + 1) * SUBLANES, p:2 * p] = hi

    y = jnp.dot(hb_ref[...].astype(BF16), cb_ref[...], preferred_element_type=F32)
    for k in range(s_len):
        for c in range(n_lb):
            o_ref[c, pl.ds(k, SUBLANES, stride=s_len), :] = y[k * SUBLANES:(k + 1) * SUBLANES,
                                                               c * LANES:(c + 1) * LANES]


def _s5_scan(uz, wb, cb, acoef, bsz, seq):
    n_chunks = wb.shape[0]
    e = n_chunks * SSM_CHUNK_IN
    n_lb = SSM_CHUNK_IN // LANES
    t_tok = min(SCAN_T, seq)
    nt = seq // t_tok
    assert seq % t_tok == 0 and t_tok % (SUBLANES * SUBLANES) == 0
    return pl.pallas_call(
        functools.partial(_s5_scan_kernel, t_tok=t_tok),
        grid=(bsz, n_chunks, nt),
        in_specs=[
            pl.BlockSpec((n_lb, t_tok, LANES), lambda b, c, t: (c, b * nt + t, 0)),
            pl.BlockSpec((None, SSM_CHUNK_IN, 2 * SSM_CHUNK_STATE), lambda b, c, t: (c, 0, 0)),
            pl.BlockSpec((None, 2 * SSM_CHUNK_STATE, SSM_CHUNK_IN), lambda b, c, t: (c, 0, 0)),
            pl.BlockSpec((None, SUBLANES, SSM_CHUNK_STATE), lambda b, c, t: (c, 0, 0)),
        ],
        out_specs=pl.BlockSpec((n_lb, t_tok, LANES), lambda b, c, t: (c, b * nt + t, 0)),
        out_shape=jax.ShapeDtypeStruct((e // LANES, bsz * seq, LANES), F32),
        scratch_shapes=[pltpu.VMEM((t_tok, 2 * SSM_CHUNK_STATE), F32),
                        pltpu.VMEM((t_tok, 2 * SSM_CHUNK_STATE), F32),
                        pltpu.VMEM((SUBLANES, 2 * SSM_CHUNK_STATE), F32)],
        compiler_params=_params("parallel", "parallel", "arbitrary"),
        name="s5_scan",
    )(uz, wb, cb, acoef)


def _s5_constants(lam_re, lam_im, log_dt, b_re, b_im, c_re, c_im, s_len):
    lr = lam_re.astype(F32)
    li = lam_im.astype(F32)
    dt = jnp.exp(log_dt.astype(F32))[:, None]
    mag = jnp.exp(lr * dt)
    ar = mag * jnp.cos(li * dt)
    ai = mag * jnp.sin(li * dt)
    inv_abs2 = 1.0 / (lr * lr + li * li)
    cr = ((ar - 1.0) * lr + ai * li) * inv_abs2
    ci = (ai * lr - (ar - 1.0) * li) * inv_abs2
    br = b_re.astype(F32)
    bi = b_im.astype(F32)
    bbar_r = cr[..., None] * br - ci[..., None] * bi
    bbar_i = cr[..., None] * bi + ci[..., None] * br
    g = lr.shape[0]
    nck = g // SSM_CHUNK_GROUPS
    eye = jnp.eye(SSM_CHUNK_GROUPS, dtype=F32)

    def in_blocks(w):
        w = w.reshape(nck, SSM_CHUNK_GROUPS, SSM_STATE, SSM_GROUP)
        return jnp.einsum('kgpc,gh->kgchp', w, eye).reshape(nck, SSM_CHUNK_IN, SSM_CHUNK_STATE)

    def out_blocks(w):
        w = w.reshape(nck, SSM_CHUNK_GROUPS, SSM_GROUP, SSM_STATE)
        return jnp.einsum('kgcp,gh->kgphc', w, eye).reshape(nck, SSM_CHUNK_STATE, SSM_CHUNK_IN)

    wb = jnp.concatenate([in_blocks(bbar_r), in_blocks(bbar_i)], axis=2).astype(BF16)
    cb = jnp.concatenate([out_blocks(c_re.astype(F32)), -out_blocks(c_im.astype(F32))], axis=1).astype(BF16)
    sr, si = ar, ai
    for _ in range(int(math.log2(s_len))):
        sr, si = sr * sr - si * si, 2.0 * sr * si
    rows = [v.reshape(nck, 1, SSM_CHUNK_STATE) for v in (ar, ai, sr, si)]
    rows += [jnp.zeros_like(rows[0])] * (SUBLANES - len(rows))
    return wb, cb, jnp.concatenate(rows, axis=1)


def _s5_post_kernel(y_ref, u_ref, z_ref, x_ref, d_ref, wg_ref, bg_ref, wo_ref, lg_ref, lb_ref,
                    xo_ref, xb_ref, *, alpha):
    unblock = lambda ref: jnp.concatenate([ref[j] for j in range(ref.shape[0])], axis=1)
    y = unblock(y_ref) + d_ref[...] * unblock(u_ref)
    g = jax.nn.gelu(y)
    gl = jnp.dot(g.astype(BF16), wg_ref[...], preferred_element_type=F32) + bg_ref[...]
    y2 = g * _sigmoid(gl)
    z = unblock(z_ref)
    a = (y2 * (z * _sigmoid(z))).astype(BF16)
    o = jnp.dot(a, wo_ref[...], preferred_element_type=F32)
    xn = _layer_norm(alpha * x_ref[...] + o, lg_ref[...], lb_ref[...])
    xo_ref[...] = xn
    xb_ref[...] = xn.astype(BF16)


def _s5_post(yssm, uz, x, d, w_glu, b_glu, w_out, ln_g, ln_b, alpha, tm=256):
    nlb, m, _ = yssm.shape
    e = nlb * LANES
    dm = x.shape[1]
    tm = min(tm, m)
    row = lambda i: (i, 0)
    fixed = lambda i: (0, 0)
    return pl.pallas_call(
        functools.partial(_s5_post_kernel, alpha=alpha),
        grid=(m // tm,),
        in_specs=[pl.BlockSpec((nlb, tm, LANES), lambda i: (0, i, 0)),
                  pl.BlockSpec((nlb, tm, LANES), lambda i: (0, i, 0)),
                  pl.BlockSpec((nlb, tm, LANES), lambda i: (1, i, 0)),
                  pl.BlockSpec((tm, dm), row),
                  pl.BlockSpec((1, e), fixed),
                  pl.BlockSpec((e, e), fixed),
                  pl.BlockSpec((1, e), fixed),
                  pl.BlockSpec((e, dm), fixed),
                  pl.BlockSpec((1, dm), fixed),
                  pl.BlockSpec((1, dm), fixed)],
        out_specs=[pl.BlockSpec((tm, dm), row), pl.BlockSpec((tm, dm), row)],
        out_shape=[jax.ShapeDtypeStruct((m, dm), F32), jax.ShapeDtypeStruct((m, dm), BF16)],
        compiler_params=_params("parallel"),
        name="s5_post",
    )(yssm, uz, uz, x, d, w_glu, b_glu, w_out, ln_g, ln_b)


def _compress_kernel(t_ref, pos_ref, w1_ref, w2_ref, o_ref, q_ref, *, nc):
    half = CMP_BLOCK // 2
    acc_p = jnp.zeros((nc, HEAD_DIM), F32)
    acc_q = jnp.zeros((nc, HEAD_DIM), F32)
    for l in range(half):
        t = t_ref[pl.ds(l, nc, stride=CMP_STRIDE), :]
        lo = (t + pos_ref[l:l + 1, :]).astype(BF16)
        hi = (t + pos_ref[half + l:half + l + 1, :]).astype(BF16)
        acc_p += jnp.dot(lo, w1_ref[l * HEAD_DIM:(l + 1) * HEAD_DIM, :], preferred_element_type=F32)
        acc_q += jnp.dot(hi, w1_ref[(half + l) * HEAD_DIM:(half + l + 1) * HEAD_DIM, :],
                         preferred_element_type=F32)
    q_ref[0:nc, :] = acc_q
    q_ref[nc:nc + SUBLANES, :] = jnp.zeros((SUBLANES, HEAD_DIM), F32)
    pre = acc_p + q_ref[pl.ds(1, nc), :]
    mid = jax.nn.gelu(pre).astype(BF16)
    o_ref[...] = jnp.dot(mid, w2_ref[...], preferred_element_type=F32).astype(o_ref.dtype)


def _compress(kvc, pos, w1, w2, bsz, seq):
    nc = seq // CMP_STRIDE
    g = N_KV_HEADS
    return pl.pallas_call(
        functools.partial(_compress_kernel, nc=nc),
        grid=(bsz, 2, g),
        in_specs=[pl.BlockSpec((seq, HEAD_DIM), lambda b, s, h: (b, s * g + h)),
                  pl.BlockSpec((None, CMP_BLOCK, HEAD_DIM), lambda b, s, h: (s, 0, 0)),
                  pl.BlockSpec((None, CMP_BLOCK * HEAD_DIM, HEAD_DIM), lambda b, s, h: (s, 0, 0)),
                  pl.BlockSpec((None, HEAD_DIM, HEAD_DIM), lambda b, s, h: (s, 0, 0))],
        out_specs=pl.BlockSpec((None, None, None, nc, HEAD_DIM), lambda b, s, h: (b, s, h, 0, 0)),
        out_shape=jax.ShapeDtypeStruct((bsz, 2, g, nc, HEAD_DIM), BF16),
        scratch_shapes=[pltpu.VMEM((nc + SUBLANES, HEAD_DIM), F32)],
        compiler_params=_params("parallel", "parallel", "parallel"),
        name="kv_compress",
    )(kvc, pos, w1, w2)


def _cmp_select_kernel(q_ref, k_ref, v_ref, o_ref, sel_ref, *, tq, nc, ns, n_sel):
    q0 = pl.program_id(2) * tq
    k = k_ref[...]
    v = v_ref[...]
    t = q0 + lax.broadcasted_iota(jnp.int32, (tq, nc), 0)
    n = lax.broadcasted_iota(jnp.int32, (tq, nc), 1)
    valid = n * CMP_STRIDE + (CMP_BLOCK - 1) <= t
    psum = jnp.zeros((tq, nc), F32)
    for h in range(HEADS_PER_KV):
        qh = q_ref[:, h * HEAD_DIM:(h + 1) * HEAD_DIM]
        s = lax.dot_general(qh, k, (((1,), (1,)), ((), ())), preferred_element_type=F32)
        s = jnp.where(valid, s, MASK_VALUE)
        m = jnp.max(s, axis=-1, keepdims=True)
        e = jnp.where(valid, jnp.exp(s - m), 0.0)
        p = e / jnp.maximum(jnp.sum(e, axis=-1, keepdims=True), 1e-30)
        o_ref[:, h * HEAD_DIM:(h + 1) * HEAD_DIM] = jnp.dot(p.astype(BF16), v, preferred_element_type=F32)
        psum = psum + p

    nn = lax.broadcasted_iota(jnp.int32, (nc, LANES), 0)
    jj = lax.broadcasted_iota(jnp.int32, (nc, LANES), 1)
    overlap = ((nn * CMP_STRIDE < (jj + 1) * SEL_BLOCK)
               & (nn * CMP_STRIDE + (CMP_BLOCK - 1) >= jj * SEL_BLOCK)
               & (jj < ns)).astype(BF16)
    p_hi = psum.astype(BF16)
    p_lo = (psum - p_hi.astype(F32)).astype(BF16)
    imp = (jnp.dot(p_hi, overlap, preferred_element_type=F32)
           + jnp.dot(p_lo, overlap, preferred_element_type=F32))

    tt = q0 + lax.broadcasted_iota(jnp.int32, (tq, LANES), 0)
    j = lax.broadcasted_iota(jnp.int32, (tq, LANES), 1)
    cur = tt // SEL_BLOCK
    forced = (j == 0) | (j == cur) | (j == cur - 1)
    imp = jnp.where(j > cur, MASK_VALUE, jnp.where(forced, FORCE_SCORE, imp))
    imp = jnp.where(j < ns, imp, PAD_SCORE)

    x = imp.T[0:ns, :]
    jidx = lax.broadcasted_iota(jnp.int32, (ns, tq), 0)
    cnt = jnp.zeros((ns, tq), F32)
    for i in range(ns):
        xi = x[i:i + 1, :]
        beats = (xi > x) | ((xi == x) & (jidx > i))
        cnt = cnt + beats.astype(F32)
    sel_t = (cnt < n_sel).astype(F32)
    if ns < LANES:
        sel_t = jnp.concatenate([sel_t, jnp.zeros((LANES - ns, tq), F32)], axis=0)
    sel_ref[...] = sel_t.T.astype(sel_ref.dtype)


def _cmp_select(q, cmp_kv, bsz, seq):
    tq = min(ATT_TQ, seq)
    nq = seq // tq
    nc = seq // CMP_STRIDE
    ns = seq // SEL_BLOCK
    assert ns <= LANES and tq == LANES
    g = N_KV_HEADS
    return pl.pallas_call(
        functools.partial(_cmp_select_kernel, tq=tq, nc=nc, ns=ns, n_sel=min(N_SELECT, ns)),
        grid=(bsz, g, nq),
        in_specs=[pl.BlockSpec((tq, GROUP_WIDTH), lambda b, h, i: (b * nq + i, h)),
                  pl.BlockSpec((None, None, None, nc, HEAD_DIM), lambda b, h, i: (b, 0, h, 0, 0)),
                  pl.BlockSpec((None, None, None, nc, HEAD_DIM), lambda b, h, i: (b, 1, h, 0, 0))],
        out_specs=[pl.BlockSpec((tq, GROUP_WIDTH), lambda b, h, i: (b * nq + i, h)),
                   pl.BlockSpec((None, None, tq, LANES), lambda b, h, i: (b, h, i, 0))],
        out_shape=[jax.ShapeDtypeStruct((bsz * seq, N_HEADS * HEAD_DIM), F32),
                   jax.ShapeDtypeStruct((bsz, g, seq, LANES), BF16)],
        compiler_params=_params("parallel", "parallel", "parallel"),
        name="nsa_cmp_select",
    )(q, cmp_kv, cmp_kv)


def _slc_win_kernel(q_ref, sel_ref, ks_ref, vs_ref, kw_ref, vw_ref, os_ref, ow_ref, *, tq, tk, seq):
    hp = HEADS_PER_KV
    q0 = pl.program_id(2) * tq
    q4 = jnp.concatenate([q_ref[:, h * HEAD_DIM:(h + 1) * HEAD_DIM] for h in range(hp)], axis=0)
    t = q0 + lax.broadcasted_iota(jnp.int32, (tq, tk), 0)
    col = lax.broadcasted_iota(jnp.int32, (tq, tk), 1)
    neg = ((sel_ref[...].astype(F32) - 1.0) * (-MASK_VALUE)).astype(BF16)
    jj = lax.broadcasted_iota(jnp.int32, (LANES, tk), 0)
    cc = lax.broadcasted_iota(jnp.int32, (LANES, tk), 1)

    def body(kt, carry):
        m, l, acc = carry
        k0 = pl.multiple_of(kt * tk, tk)
        expand = ((k0 + cc) // SEL_BLOCK == jj).astype(BF16)
        bias = jnp.dot(neg, expand, preferred_element_type=F32)
        bias = jnp.where(k0 + col <= t, bias, MASK_VALUE)
        kt_ = ks_ref[pl.ds(k0, tk), :]
        vt_ = vs_ref[pl.ds(k0, tk), :]
        s = lax.dot_general(q4, kt_, (((1,), (1,)), ((), ())), preferred_element_type=F32)
        s = (s.reshape(hp, tq, tk) + bias[None]).reshape(hp * tq, tk)
        m_new = jnp.maximum(m, jnp.max(s, axis=-1, keepdims=True))
        alpha = jnp.exp(m - m_new)
        e = jnp.exp(s - m_new)
        l = alpha * l + jnp.sum(e, axis=-1, keepdims=True)
        acc = alpha * acc + jnp.dot(e.astype(BF16), vt_, preferred_element_type=F32)
        return m_new, l, acc

    n_tiles = (q0 + tq - 1) // tk + 1
    init = (jnp.full((hp * tq, 1), MASK_VALUE, F32), jnp.zeros((hp * tq, 1), F32),
            jnp.zeros((hp * tq, HEAD_DIM), F32))
    m, l, acc = lax.fori_loop(0, n_tiles, body, init)
    o = acc / l
    for h in range(hp):
        os_ref[:, h * HEAD_DIM:(h + 1) * HEAD_DIM] = o[h * tq:(h + 1) * tq, :]

    span = min(WINDOW + tq, seq)
    start = pl.multiple_of(jnp.clip(q0 + tq - span, 0, seq - span), tq)
    kpos = start + lax.broadcasted_iota(jnp.int32, (tq, span), 1)
    tw = q0 + lax.broadcasted_iota(jnp.int32, (tq, span), 0)
    wbias = jnp.where((kpos <= tw) & (kpos > tw - WINDOW), 0.0, MASK_VALUE)
    kw = kw_ref[pl.ds(start, span), :]
    vw = vw_ref[pl.ds(start, span), :]
    s = lax.dot_general(q4, kw, (((1,), (1,)), ((), ())), preferred_element_type=F32)
    s = (s.reshape(hp, tq, span) + wbias[None]).reshape(hp * tq, span)
    mw = jnp.max(s, axis=-1, keepdims=True)
    e = jnp.exp(s - mw)
    p = e / jnp.sum(e, axis=-1, keepdims=True)
    ow = jnp.dot(p.astype(BF16), vw, preferred_element_type=F32)
    for h in range(hp):
        ow_ref[:, h * HEAD_DIM:(h + 1) * HEAD_DIM] = ow[h * tq:(h + 1) * tq, :]


def _slc_win(q, sel, kvs, bsz, seq):
    tq = min(ATT_TQ, seq)
    tk = min(ATT_TK, seq)
    nq = seq // tq
    g = N_KV_HEADS
    kv_spec = lambda part: pl.BlockSpec((seq, HEAD_DIM), lambda b, h, i: (b, part * g + h))
    out_spec = pl.BlockSpec((tq, GROUP_WIDTH), lambda b, h, i: (b * nq + i, h))
    out_sds = jax.ShapeDtypeStruct((bsz * seq, N_HEADS * HEAD_DIM), F32)
    return pl.pallas_call(
        functools.partial(_slc_win_kernel, tq=tq, tk=tk, seq=seq),
        grid=(bsz, g, nq),
        in_specs=[pl.BlockSpec((tq, GROUP_WIDTH), lambda b, h, i: (b * nq + i, h)),
                  pl.BlockSpec((None, None, tq, LANES), lambda b, h, i: (b, h, i, 0)),
                  kv_spec(0), kv_spec(1), kv_spec(2), kv_spec(3)],
        out_specs=[out_spec, out_spec],
        out_shape=[out_sds, out_sds],
        compiler_params=_params("parallel", "parallel", "parallel"),
        name="nsa_slc_win",
    )(q, sel, kvs, kvs, kvs, kvs)


def _nsa_post_kernel(oc_ref, os_ref, ow_ref, z_ref, gl_ref, x_ref, wo_ref, lg_ref, lb_ref,
                     xo_ref, xb_ref, *, alpha):
    width = N_HEADS * HEAD_DIM
    gate = _sigmoid(gl_ref[...])
    parts = []
    for h in range(N_HEADS):
        hs = slice(h * HEAD_DIM, (h + 1) * HEAD_DIM)
        acc = None
        for br, o_ref in enumerate((oc_ref, os_ref, ow_ref)):
            gcol = gate[:, br * N_HEADS + h:br * N_HEADS + h + 1]
            z = z_ref[:, br * width + h * HEAD_DIM:br * width + (h + 1) * HEAD_DIM]
            term = gcol * o_ref[:, hs] * (z * _sigmoid(z))
            acc = term if acc is None else acc + term
        parts.append(acc.astype(BF16))
    y = jnp.concatenate(parts, axis=1)
    o = jnp.dot(y, wo_ref[...], preferred_element_type=F32)
    xn = _layer_norm(alpha * x_ref[...] + o, lg_ref[...], lb_ref[...])
    xo_ref[...] = xn
    xb_ref[...] = xn.astype(BF16)


def _nsa_post(o_cmp, o_slc, o_win, z, gate_logits, x, w_out, ln_g, ln_b, alpha, tm=128):
    m, dm = x.shape
    width = N_HEADS * HEAD_DIM
    tm = min(tm, m)
    row = lambda i: (i, 0)
    fixed = lambda i: (0, 0)
    return pl.pallas_call(
        functools.partial(_nsa_post_kernel, alpha=alpha),
        grid=(m // tm,),
        in_specs=[pl.BlockSpec((tm, width), row),
                  pl.BlockSpec((tm, width), row),
                  pl.BlockSpec((tm, width), row),
                  pl.BlockSpec((tm, N_BRANCH * width), row),
                  pl.BlockSpec((tm, LANES), row),
                  pl.BlockSpec((tm, dm), row),
                  pl.BlockSpec((width, dm), fixed),
                  pl.BlockSpec((1, dm), fixed),
                  pl.BlockSpec((1, dm), fixed)],
        out_specs=[pl.BlockSpec((tm, dm), row), pl.BlockSpec((tm, dm), row)],
        out_shape=[jax.ShapeDtypeStruct((m, dm), F32), jax.ShapeDtypeStruct((m, dm), BF16)],
        compiler_params=_params("parallel"),
        name="nsa_post",
    )(o_cmp, o_slc, o_win, z, gate_logits, x, w_out, ln_g, ln_b)


def kernel(x, a_w_in, a_lam_re, a_lam_im, a_log_dt, a_b_re, a_b_im, a_c_re, a_c_im, a_d, a_w_glu, a_b_glu, a_w_out, kv_w, cmp_pos_k, cmp_w1_k, cmp_w2_k, cmp_pos_v, cmp_w1_v, cmp_w2_v, b_w_in, b_w_out, ln_g, ln_b):
    bsz, seq, dm = x.shape
    n_a = a_w_in.shape[0]
    n_b = b_w_in.shape[0]
    depth = n_a + n_b
    alpha = (2 * depth) ** 0.25
    width = N_HEADS * HEAD_DIM
    kvw = N_KV_HEADS * HEAD_DIM
    s_len = min(SCAN_T, seq) // SUBLANES

    xf = x.reshape(bsz * seq, dm).astype(F32)
    xb = xf.astype(BF16)

    for i in range(n_a):
        wb, cb, acoef = _s5_constants(a_lam_re[i], a_lam_im[i], a_log_dt[i], a_b_re[i], a_b_im[i],
                                      a_c_re[i], a_c_im[i], s_len)
        e = a_w_glu.shape[1]
        uz = _matmul(xb, a_w_in[i].astype(BF16), F32, lane_blocked=True)
        yssm = _s5_scan(uz, wb, cb, acoef, bsz, seq)
        xf, xb = _s5_post(yssm, uz, xf, a_d[i].reshape(1, e).astype(F32), a_w_glu[i].astype(BF16),
                          a_b_glu[i].reshape(1, e).astype(F32), a_w_out[i].astype(BF16),
                          ln_g[i].reshape(1, dm).astype(F32), ln_b[i].reshape(1, dm).astype(F32), alpha)

    kvc = _matmul(xb, kv_w[:, :2 * kvw].astype(BF16), F32)
    kvs = _matmul(xb, kv_w[:, 2 * kvw:].astype(BF16), BF16)
    cmp_kv = _compress(kvc,
                       jnp.stack([cmp_pos_k, cmp_pos_v]).astype(F32),
                       jnp.stack([cmp_w1_k, cmp_w1_v]).astype(BF16),
                       jnp.stack([cmp_w2_k, cmp_w2_v]).astype(BF16), bsz, seq)

    for i in range(n_b):
        layer = n_a + i
        w_in = b_w_in[i]
        n_gate = w_in.shape[1] - (1 + N_BRANCH) * width
        w_gate = jnp.pad(w_in[:, (1 + N_BRANCH) * width:], ((0, 0), (0, LANES - n_gate)))
        q = _matmul(xb, w_in[:, :width].astype(BF16), BF16, scale=HEAD_DIM ** -0.5)
        z = _matmul(xb, w_in[:, width:(1 + N_BRANCH) * width].astype(BF16), F32)
        gate_logits = _matmul(xb, w_gate.astype(BF16), F32)
        o_cmp, sel = _cmp_select(q, cmp_kv, bsz, seq)
        o_slc, o_win = _slc_win(q, sel, kvs, bsz, seq)
        xf, xb = _nsa_post(o_cmp, o_slc, o_win, z, gate_logits, xf, b_w_out[i].astype(BF16),
                           ln_g[layer].reshape(1, dm).astype(F32), ln_b[layer].reshape(1, dm).astype(F32), alpha)

    return xf.reshape(bsz, seq, dm).astype(x.dtype)
```

```python
import functools
import math

import jax
import jax.numpy as jnp
from jax import lax
from jax.experimental import pallas as pl
from jax.experimental.pallas import tpu as pltpu

F32 = jnp.float32
BF16 = jnp.bfloat16

HEAD_DIM = 128
N_HEADS = 16
N_KV_HEADS = 4
HEADS_PER_KV = N_HEADS // N_KV_HEADS
GROUP_WIDTH = HEADS_PER_KV * HEAD_DIM
N_BRANCH = 3
CMP_BLOCK = 32
CMP_STRIDE = 16
SEL_BLOCK = 64
N_SELECT = 16
WINDOW = 512
SSM_GROUP = 16
SSM_STATE = 64
SSM_CHUNK_GROUPS = 16
SSM_CHUNK_IN = SSM_CHUNK_GROUPS * SSM_GROUP
SSM_CHUNK_STATE = SSM_CHUNK_GROUPS * SSM_STATE
LN_EPS = 1e-5
MASK_VALUE = -1e30
FORCE_SCORE = 1e6
PAD_SCORE = -3e38
LOG2_E = math.log2(math.e)

SUBLANES = 8
LANES = 128
VMEM_LIMIT_BYTES = 48 * 1024 * 1024

SCAN_T = 256
CMP_TQ = 256
ATT_TQ = 256
ATT_TK = 512


def _params(*sem):
    return pltpu.CompilerParams(dimension_semantics=sem, vmem_limit_bytes=VMEM_LIMIT_BYTES)


def _sigmoid(v):
    return 1.0 / (1.0 + jnp.exp(-v))


def _layer_norm(v, g, b):
    mu = jnp.mean(v, axis=-1, keepdims=True)
    d = v - mu
    var = jnp.mean(d * d, axis=-1, keepdims=True)
    return d * lax.rsqrt(var + LN_EPS) * g + b


def _mm_kernel(a_ref, w_ref, o_ref, *, scale, lane_blocked):
    acc = jnp.dot(a_ref[...], w_ref[...], preferred_element_type=F32)
    if scale is not None:
        acc = acc * scale
    if lane_blocked:
        for j in range(o_ref.shape[0]):
            o_ref[j] = acc[:, j * LANES:(j + 1) * LANES].astype(o_ref.dtype)
    else:
        o_ref[...] = acc.astype(o_ref.dtype)


def _matmul(a, w, out_dtype, *, tm=512, tn=1024, scale=None, lane_blocked=False):
    m, k = a.shape
    n = w.shape[1]
    tm = min(tm, m)
    tn = min(tn, n)
    assert m % tm == 0 and n % tn == 0
    if lane_blocked:
        out_spec = pl.BlockSpec((tn // LANES, tm, LANES), lambda j, i: (j, i, 0))
        out_shape = jax.ShapeDtypeStruct((n // LANES, m, LANES), out_dtype)
    else:
        out_spec = pl.BlockSpec((tm, tn), lambda j, i: (i, j))
        out_shape = jax.ShapeDtypeStruct((m, n), out_dtype)
    return pl.pallas_call(
        functools.partial(_mm_kernel, scale=scale, lane_blocked=lane_blocked),
        grid=(n // tn, m // tm),
        in_specs=[pl.BlockSpec((tm, k), lambda j, i: (i, 0)),
                  pl.BlockSpec((k, tn), lambda j, i: (0, j))],
        out_specs=out_spec,
        out_shape=out_shape,
        compiler_params=_params("parallel", "parallel"),
        name="matmul",
    )(a, w)


def _s5_scan_kernel(u_ref, wb_ref, cb_ref, a_ref, o_ref, bu_ref, hb_ref, carry_ref, *, t_tok):
    s_len = t_tok // SUBLANES
    p = SSM_CHUNK_STATE

    @pl.when(pl.program_id(2) == 0)
    def _():
        carry_ref[...] = jnp.zeros_like(carry_ref)

    n_lb = SSM_CHUNK_IN // LANES
    rows = [jnp.concatenate([u_ref[c, pl.ds(k, SUBLANES, stride=s_len), :] for c in range(n_lb)], axis=1)
            for k in range(s_len)]
    up = jnp.concatenate(rows, axis=0).astype(BF16)
    bu_ref[...] = jnp.dot(up, wb_ref[...], preferred_element_type=F32)

    ar = jnp.broadcast_to(a_ref[0:1, :], (SUBLANES, p))
    ai = jnp.broadcast_to(a_ref[1:2, :], (SUBLANES, p))

    def step(k, hr, hi):
        br = bu_ref[k * SUBLANES:(k + 1) * SUBLANES, 0:p]
        bi = bu_ref[k * SUBLANES:(k + 1) * SUBLANES, p:2 * p]
        return ar * hr - ai * hi + br, ar * hi + ai * hr + bi

    er = jnp.zeros((SUBLANES, p), F32)
    ei = jnp.zeros((SUBLANES, p), F32)
    for k in range(s_len):
        er, ei = step(k, er, ei)

    sr = jnp.broadcast_to(a_ref[2:3, :], (SUBLANES, p))
    si = jnp.broadcast_to(a_ref[3:4, :], (SUBLANES, p))
    sub = lax.broadcasted_iota(jnp.int32, (SUBLANES, p), 0)
    hinr = jnp.where(sub == 0, carry_ref[:, 0:p], 0.0)
    hini = jnp.where(sub == 0, carry_ref[:, p:2 * p], 0.0)
    for j in range(SUBLANES - 1):
        nr = sr * hinr - si * hini + er
        ni = sr * hini + si * hinr + ei
        hinr = jnp.where(sub == j + 1, pltpu.roll(nr, 1, 0), hinr)
        hini = jnp.where(sub == j + 1, pltpu.roll(ni, 1, 0), hini)
    nr = sr * hinr - si * hini + er
    ni = sr * hini + si * hinr + ei
    carry_ref[:, 0:p] = jnp.broadcast_to(nr[SUBLANES - 1:SUBLANES, :], (SUBLANES, p))
    carry_ref[:, p:2 * p] = jnp.broadcast_to(ni[SUBLANES - 1:SUBLANES, :], (SUBLANES, p))

    hr, hi = hinr, hini
    for k in range(s_len):
        hr, hi = step(k, hr, hi)
        hb_ref[k * SUBLANES:(k + 1) * SUBLANES, 0:p] = hr
        hb_ref[k * SUBLANES:(k + 1) * SUBLANES, p:2 * p] = hi

    y = jnp.dot(hb_ref[...].astype(BF16), cb_ref[...], preferred_element_type=F32)
    for k in range(s_len):
        for c in range(n_lb):
            o_ref[c, pl.ds(k, SUBLANES, stride=s_len), :] = y[k * SUBLANES:(k + 1) * SUBLANES,
                                                               c * LANES:(c + 1) * LANES]


def _s5_scan(uz, wb, cb, acoef, bsz, seq):
    n_chunks = wb.shape[0]
    e = n_chunks * SSM_CHUNK_IN
    n_lb = SSM_CHUNK_IN // LANES
    t_tok = min(SCAN_T, seq)
    nt = seq // t_tok
    assert seq % t_tok == 0 and t_tok % (SUBLANES * SUBLANES) == 0
    return pl.pallas_call(
        functools.partial(_s5_scan_kernel, t_tok=t_tok),
        grid=(bsz, n_chunks, nt),
        in_specs=[
            pl.BlockSpec((n_lb, t_tok, LANES), lambda b, c, t: (c, b * nt + t, 0)),
            pl.BlockSpec((None, SSM_CHUNK_IN, 2 * SSM_CHUNK_STATE), lambda b, c, t: (c, 0, 0)),
            pl.BlockSpec((None, 2 * SSM_CHUNK_STATE, SSM_CHUNK_IN), lambda b, c, t: (c, 0, 0)),
            pl.BlockSpec((None, SUBLANES, SSM_CHUNK_STATE), lambda b, c, t: (c, 0, 0)),
        ],
        out_specs=pl.BlockSpec((n_lb, t_tok, LANES), lambda b, c, t: (c, b * nt + t, 0)),
        out_shape=jax.ShapeDtypeStruct((e // LANES, bsz * seq, LANES), F32),
        scratch_shapes=[pltpu.VMEM((t_tok, 2 * SSM_CHUNK_STATE), F32),
                        pltpu.VMEM((t_tok, 2 * SSM_CHUNK_STATE), F32),
                        pltpu.VMEM((SUBLANES, 2 * SSM_CHUNK_STATE), F32)],
        compiler_params=_params("parallel", "parallel", "arbitrary"),
        name="s5_scan",
    )(uz, wb, cb, acoef)


def _s5_constants(lam_re, lam_im, log_dt, b_re, b_im, c_re, c_im, s_len):
    lr = lam_re.astype(F32)
    li = lam_im.astype(F32)
    dt = jnp.exp(log_dt.astype(F32))[:, None]
    mag = jnp.exp(lr * dt)
    ar = mag * jnp.cos(li * dt)
    ai = mag * jnp.sin(li * dt)
    inv_abs2 = 1.0 / (lr * lr + li * li)
    cr = ((ar - 1.0) * lr + ai * li) * inv_abs2
    ci = (ai * lr - (ar - 1.0) * li) * inv_abs2
    br = b_re.astype(F32)
    bi = b_im.astype(F32)
    bbar_r = cr[..., None] * br - ci[..., None] * bi
    bbar_i = cr[..., None] * bi + ci[..., None] * br
    g = lr.shape[0]
    nck = g // SSM_CHUNK_GROUPS
    eye = jnp.eye(SSM_CHUNK_GROUPS, dtype=F32)

    def in_blocks(w):
        w = w.reshape(nck, SSM_CHUNK_GROUPS, SSM_STATE, SSM_GROUP)
        return jnp.einsum('kgpc,gh->kgchp', w, eye).reshape(nck, SSM_CHUNK_IN, SSM_CHUNK_STATE)

    def out_blocks(w):
        w = w.reshape(nck, SSM_CHUNK_GROUPS, SSM_GROUP, SSM_STATE)
        return jnp.einsum('kgcp,gh->kgphc', w, eye).reshape(nck, SSM_CHUNK_STATE, SSM_CHUNK_IN)

    wb = jnp.concatenate([in_blocks(bbar_r), in_blocks(bbar_i)], axis=2).astype(BF16)
    cb = jnp.concatenate([out_blocks(c_re.astype(F32)), -out_blocks(c_im.astype(F32))], axis=1).astype(BF16)
    sr, si = ar, ai
    for _ in range(int(math.log2(s_len))):
        sr, si = sr * sr - si * si, 2.0 * sr * si
    rows = [v.reshape(nck, 1, SSM_CHUNK_STATE) for v in (ar, ai, sr, si)]
    rows += [jnp.zeros_like(rows[0])] * (SUBLANES - len(rows))
    return wb, cb, jnp.concatenate(rows, axis=1)


def _s5_post_kernel(y_ref, u_ref, z_ref, x_ref, d_ref, wg_ref, bg_ref, wo_ref, lg_ref, lb_ref,
                    xo_ref, xb_ref, *, alpha):
    unblock = lambda ref: jnp.concatenate([ref[j] for j in range(ref.shape[0])], axis=1)
    y = unblock(y_ref) + d_ref[...] * unblock(u_ref)
    g = jax.nn.gelu(y)
    gl = jnp.dot(g.astype(BF16), wg_ref[...], preferred_element_type=F32) + bg_ref[...]
    y2 = g * _sigmoid(gl)
    z = unblock(z_ref)
    a = (y2 * (z * _sigmoid(z))).astype(BF16)
    o = jnp.dot(a, wo_ref[...], preferred_element_type=F32)
    xn = _layer_norm(alpha * x_ref[...] + o, lg_ref[...], lb_ref[...])
    xo_ref[...] = xn
    xb_ref[...] = xn.astype(BF16)


def _s5_post(yssm, uz, x, d, w_glu, b_glu, w_out, ln_g, ln_b, alpha, tm=256):
    nlb, m, _ = yssm.shape
    e = nlb * LANES
    dm = x.shape[1]
    tm = min(tm, m)
    row = lambda i: (i, 0)
    fixed = lambda i: (0, 0)
    return pl.pallas_call(
        functools.partial(_s5_post_kernel, alpha=alpha),
        grid=(m // tm,),
        in_specs=[pl.BlockSpec((nlb, tm, LANES), lambda i: (0, i, 0)),
                  pl.BlockSpec((nlb, tm, LANES), lambda i: (0, i, 0)),
                  pl.BlockSpec((nlb, tm, LANES), lambda i: (1, i, 0)),
                  pl.BlockSpec((tm, dm), row),
                  pl.BlockSpec((1, e), fixed),
                  pl.BlockSpec((e, e), fixed),
                  pl.BlockSpec((1, e), fixed),
                  pl.BlockSpec((e, dm), fixed),
                  pl.BlockSpec((1, dm), fixed),
                  pl.BlockSpec((1, dm), fixed)],
        out_specs=[pl.BlockSpec((tm, dm), row), pl.BlockSpec((tm, dm), row)],
        out_shape=[jax.ShapeDtypeStruct((m, dm), F32), jax.ShapeDtypeStruct((m, dm), BF16)],
        compiler_params=_params("parallel"),
        name="s5_post",
    )(yssm, uz, uz, x, d, w_glu, b_glu, w_out, ln_g, ln_b)


def _compress_kernel(t_ref, pos_ref, w1_ref, w2_ref, o_ref, q_ref, *, nc):
    half = CMP_BLOCK // 2
    acc_p = jnp.zeros((nc, HEAD_DIM), F32)
    acc_q = jnp.zeros((nc, HEAD_DIM), F32)
    for l in range(half):
        t = t_ref[pl.ds(l, nc, stride=CMP_STRIDE), :]
        lo = (t + pos_ref[l:l + 1, :]).astype(BF16)
        hi = (t + pos_ref[half + l:half + l + 1, :]).astype(BF16)
        acc_p += jnp.dot(lo, w1_ref[l * HEAD_DIM:(l + 1) * HEAD_DIM, :], preferred_element_type=F32)
        acc_q += jnp.dot(hi, w1_ref[(half + l) * HEAD_DIM:(half + l + 1) * HEAD_DIM, :],
                         preferred_element_type=F32)
    q_ref[0:nc, :] = acc_q
    q_ref[nc:nc + SUBLANES, :] = jnp.zeros((SUBLANES, HEAD_DIM), F32)
    pre = acc_p + q_ref[pl.ds(1, nc), :]
    mid = jax.nn.gelu(pre).astype(BF16)
    o_ref[...] = jnp.dot(mid, w2_ref[...], preferred_element_type=F32).astype(o_ref.dtype)


def _compress(kvc, pos, w1, w2, bsz, seq):
    nc = seq // CMP_STRIDE
    g = N_KV_HEADS
    return pl.pallas_call(
        functools.partial(_compress_kernel, nc=nc),
        grid=(bsz, 2, g),
        in_specs=[pl.BlockSpec((seq, HEAD_DIM), lambda b, s, h: (b, s * g + h)),
                  pl.BlockSpec((None, CMP_BLOCK, HEAD_DIM), lambda b, s, h: (s, 0, 0)),
                  pl.BlockSpec((None, CMP_BLOCK * HEAD_DIM, HEAD_DIM), lambda b, s, h: (s, 0, 0)),
                  pl.BlockSpec((None, HEAD_DIM, HEAD_DIM), lambda b, s, h: (s, 0, 0))],
        out_specs=pl.BlockSpec((None, None, None, nc, HEAD_DIM), lambda b, s, h: (b, s, h, 0, 0)),
        out_shape=jax.ShapeDtypeStruct((bsz, 2, g, nc, HEAD_DIM), BF16),
        scratch_shapes=[pltpu.VMEM((nc + SUBLANES, HEAD_DIM), F32)],
        compiler_params=_params("parallel", "parallel", "parallel"),
        name="kv_compress",
    )(kvc, pos, w1, w2)


def _cmp_select_kernel(q_ref, k_ref, v_ref, o_ref, sel_ref, *, tq, nc, ns, n_sel):
    q0 = pl.program_id(2) * tq
    k = k_ref[...]
    v = v_ref[...]
    t = q0 + lax.broadcasted_iota(jnp.int32, (tq, nc), 0)
    n = lax.broadcasted_iota(jnp.int32, (tq, nc), 1)
    valid = n * CMP_STRIDE + (CMP_BLOCK - 1) <= t
    psum = jnp.zeros((tq, nc), F32)
    for h in range(HEADS_PER_KV):
        qh = q_ref[:, h * HEAD_DIM:(h + 1) * HEAD_DIM]
        s = lax.dot_general(qh, k, (((1,), (1,)), ((), ())), preferred_element_type=F32)
        s = jnp.where(valid, s, MASK_VALUE)
        m = jnp.max(s, axis=-1, keepdims=True)
        e = jnp.where(valid, jnp.exp(s - m), 0.0)
        p = e / jnp.maximum(jnp.sum(e, axis=-1, keepdims=True), 1e-30)
        o_ref[:, h * HEAD_DIM:(h + 1) * HEAD_DIM] = jnp.dot(p.astype(BF16), v, preferred_element_type=F32)
        psum = psum + p

    nn = lax.broadcasted_iota(jnp.int32, (nc, LANES), 0)
    jj = lax.broadcasted_iota(jnp.int32, (nc, LANES), 1)
    overlap = ((nn * CMP_STRIDE < (jj + 1) * SEL_BLOCK)
               & (nn * CMP_STRIDE + (CMP_BLOCK - 1) >= jj * SEL_BLOCK)
               & (jj < ns)).astype(BF16)
    p_hi = psum.astype(BF16)
    p_lo = (psum - p_hi.astype(F32)).astype(BF16)
    imp = (jnp.dot(p_hi, overlap, preferred_element_type=F32)
           + jnp.dot(p_lo, overlap, preferred_element_type=F32))

    tt = q0 + lax.broadcasted_iota(jnp.int32, (tq, LANES), 0)
    j = lax.broadcasted_iota(jnp.int32, (tq, LANES), 1)
    cur = tt // SEL_BLOCK
    forced = (j == 0) | (j == cur) | (j == cur - 1)
    imp = jnp.where(j > cur, MASK_VALUE, jnp.where(forced, FORCE_SCORE, imp))
    imp = jnp.where(j < ns, imp, PAD_SCORE)

    x = imp.T
    nblk = ns // SUBLANES
    xs = [x[b * SUBLANES:(b + 1) * SUBLANES, :] for b in range(nblk)]
    cnts = [jnp.zeros((SUBLANES, tq), jnp.int32) for _ in range(nblk)]
    sub = lax.broadcasted_iota(jnp.int32, (SUBLANES, tq), 0)
    for i in range(ns):
        xi = x[i:i + 1, :]
        for b in range(nblk):
            if b * SUBLANES > i:
                beats = xi >= xs[b]
            elif (b + 1) * SUBLANES - 1 <= i:
                beats = xi > xs[b]
            else:
                beats = (xi > xs[b]) | ((xi >= xs[b]) & (sub > i - b * SUBLANES))
            cnts[b] = cnts[b] + beats.astype(jnp.int32)
    sel_t = jnp.concatenate([(c < n_sel).astype(F32) for c in cnts]
                            + [jnp.zeros((LANES - ns, tq), F32)] * (ns < LANES), axis=0)
    sel_ref[...] = sel_t.T.astype(sel_ref.dtype)


def _cmp_select(q, cmp_kv, bsz, seq):
    tq = min(CMP_TQ, seq)
    nq = seq // tq
    nc = seq // CMP_STRIDE
    ns = seq // SEL_BLOCK
    assert ns <= LANES and ns % SUBLANES == 0 and tq % LANES == 0
    g = N_KV_HEADS
    return pl.pallas_call(
        functools.partial(_cmp_select_kernel, tq=tq, nc=nc, ns=ns, n_sel=min(N_SELECT, ns)),
        grid=(bsz, g, nq),
        in_specs=[pl.BlockSpec((tq, GROUP_WIDTH), lambda b, h, i: (b * nq + i, h)),
                  pl.BlockSpec((None, None, None, nc, HEAD_DIM), lambda b, h, i: (b, 0, h, 0, 0)),
                  pl.BlockSpec((None, None, None, nc, HEAD_DIM), lambda b, h, i: (b, 1, h, 0, 0))],
        out_specs=[pl.BlockSpec((tq, GROUP_WIDTH), lambda b, h, i: (b * nq + i, h)),
                   pl.BlockSpec((None, None, tq, LANES), lambda b, h, i: (b, h, i, 0))],
        out_shape=[jax.ShapeDtypeStruct((bsz * seq, N_HEADS * HEAD_DIM), F32),
                   jax.ShapeDtypeStruct((bsz, g, seq, LANES), BF16)],
        compiler_params=_params("parallel", "parallel", "parallel"),
        name="nsa_cmp_select",
    )(q, cmp_kv, cmp_kv)


def _lane_chunks(v):
    return [v[:, c * LANES:(c + 1) * LANES] for c in range(v.shape[1] // LANES)]


def _slc_win_kernel(q_ref, sel_ref, oh_ref, ks_ref, vs_ref, kw_ref, vw_ref, os_ref, ow_ref,
                    s_ref, *, tq, tk, seq):
    hp = HEADS_PER_KV
    rows = hp * tq
    q0 = pl.program_id(2) * tq
    q4 = jnp.concatenate([q_ref[:, h * HEAD_DIM:(h + 1) * HEAD_DIM] for h in range(hp)], axis=0)
    neg = ((sel_ref[...].astype(F32) - 1.0) * (-MASK_VALUE)).astype(BF16)
    q_aug = jnp.concatenate([q4, jnp.concatenate([neg] * hp, axis=0)], axis=1)

    n_tiles = (q0 + tq - 1) // tk + 1
    last = n_tiles - 1
    nt_dims = (((1,), (1,)), ((), ()))

    def masked_scores(kt):
        k0 = pl.multiple_of(kt * tk, tk)
        k_aug = jnp.concatenate([ks_ref[pl.ds(k0, tk), :], oh_ref[pl.ds(k0, tk), :]], axis=1)
        s = lax.dot_general(q_aug, k_aug, nt_dims, preferred_element_type=F32)
        return s * LOG2_E

    def lane_max(s, mx):
        for sc in _lane_chunks(s):
            mx = jnp.maximum(mx, sc)
        return mx

    def scores(kt, mx):
        s = masked_scores(kt)
        s_ref[kt] = s
        return lane_max(s, mx)

    mx = lax.fori_loop(0, last, scores, jnp.full((rows, LANES), MASK_VALUE, F32))
    t = q0 + lax.broadcasted_iota(jnp.int32, (tq, tk), 0)
    kcol = last * tk + lax.broadcasted_iota(jnp.int32, (tq, tk), 1)
    s = jnp.where((kcol <= t)[None], masked_scores(last).reshape(hp, tq, tk), MASK_VALUE).reshape(rows, tk)
    s_ref[last] = s
    mx = lane_max(s, mx)
    m = jnp.broadcast_to(jnp.max(mx, axis=-1, keepdims=True), (rows, LANES))

    def weighted(kt, carry):
        ls, acc = carry
        k0 = pl.multiple_of(kt * tk, tk)
        es = [jnp.exp2(sc - m) for sc in _lane_chunks(s_ref[kt])]
        for e in es:
            ls = ls + e
        e = jnp.concatenate([e.astype(BF16) for e in es], axis=1)
        return ls, acc + jnp.dot(e, vs_ref[pl.ds(k0, tk), :], preferred_element_type=F32)

    ls, acc = lax.fori_loop(0, n_tiles, weighted,
                            (jnp.zeros((rows, LANES), F32), jnp.zeros((rows, HEAD_DIM), F32)))
    o = acc / jnp.sum(ls, axis=-1, keepdims=True)
    for h in range(hp):
        os_ref[:, h * HEAD_DIM:(h + 1) * HEAD_DIM] = o[h * tq:(h + 1) * tq, :]

    span = min(WINDOW + tq, seq)
    start = pl.multiple_of(jnp.clip(q0 + tq - span, 0, seq - span), tq)
    kpos = start + lax.broadcasted_iota(jnp.int32, (tq, span), 1)
    tw = q0 + lax.broadcasted_iota(jnp.int32, (tq, span), 0)
    wbias = jnp.where((kpos <= tw) & (kpos > tw - WINDOW), 0.0, MASK_VALUE)
    sw = lax.dot_general(q4, kw_ref[pl.ds(start, span), :], nt_dims, preferred_element_type=F32)
    sw = (sw.reshape(hp, tq, span) + wbias[None]).reshape(rows, span)
    chunks = _lane_chunks(sw)
    mw = chunks[0]
    for sc in chunks[1:]:
        mw = jnp.maximum(mw, sc)
    mw = jnp.broadcast_to(jnp.max(mw, axis=-1, keepdims=True), (rows, LANES))
    es = [jnp.exp(sc - mw) for sc in chunks]
    lw = es[0]
    for e in es[1:]:
        lw = lw + e
    e = jnp.concatenate([e.astype(BF16) for e in es], axis=1)
    ow = jnp.dot(e, vw_ref[pl.ds(start, span), :], preferred_element_type=F32)
    ow = ow / jnp.sum(lw, axis=-1, keepdims=True)
    for h in range(hp):
        ow_ref[:, h * HEAD_DIM:(h + 1) * HEAD_DIM] = ow[h * tq:(h + 1) * tq, :]


def _slc_win(q, sel, kvs, bsz, seq):
    tq = min(ATT_TQ, seq)
    tk = min(ATT_TK, seq)
    nq = seq // tq
    g = N_KV_HEADS
    block_onehot = (jnp.arange(seq)[:, None] // SEL_BLOCK == jnp.arange(LANES)[None, :]).astype(BF16)
    kv_spec = lambda part: pl.BlockSpec((seq, HEAD_DIM), lambda b, h, i: (b, part * g + h))
    out_spec = pl.BlockSpec((tq, GROUP_WIDTH), lambda b, h, i: (b * nq + i, h))
    out_sds = jax.ShapeDtypeStruct((bsz * seq, N_HEADS * HEAD_DIM), F32)
    return pl.pallas_call(
        functools.partial(_slc_win_kernel, tq=tq, tk=tk, seq=seq),
        grid=(bsz, g, nq),
        in_specs=[pl.BlockSpec((tq, GROUP_WIDTH), lambda b, h, i: (b * nq + i, h)),
                  pl.BlockSpec((None, None, tq, LANES), lambda b, h, i: (b, h, i, 0)),
                  pl.BlockSpec((seq, LANES), lambda b, h, i: (0, 0)),
                  kv_spec(0), kv_spec(1), kv_spec(2), kv_spec(3)],
        out_specs=[out_spec, out_spec],
        out_shape=[out_sds, out_sds],
        scratch_shapes=[pltpu.VMEM((seq // tk, HEADS_PER_KV * tq, tk), F32)],
        compiler_params=_params("parallel", "parallel", "parallel"),
        name="nsa_slc_win",
    )(q, sel, block_onehot, kvs, kvs, kvs, kvs)


def _nsa_post_kernel(oc_ref, os_ref, ow_ref, z_ref, gl_ref, x_ref, wo_ref, lg_ref, lb_ref,
                     xo_ref, xb_ref, *, alpha):
    width = N_HEADS * HEAD_DIM
    gate = _sigmoid(gl_ref[...])
    parts = []
    for h in range(N_HEADS):
        hs = slice(h * HEAD_DIM, (h + 1) * HEAD_DIM)
        acc = None
        for br, o_ref in enumerate((oc_ref, os_ref, ow_ref)):
            gcol = gate[:, br * N_HEADS + h:br * N_HEADS + h + 1]
            z = z_ref[:, br * width + h * HEAD_DIM:br * width + (h + 1) * HEAD_DIM]
            term = gcol * o_ref[:, hs] * (z * _sigmoid(z))
            acc = term if acc is None else acc + term
        parts.append(acc.astype(BF16))
    y = jnp.concatenate(parts, axis=1)
    o = jnp.dot(y, wo_ref[...], preferred_element_type=F32)
    xn = _layer_norm(alpha * x_ref[...] + o, lg_ref[...], lb_ref[...])
    xo_ref[...] = xn
    xb_ref[...] = xn.astype(BF16)


def _nsa_post(o_cmp, o_slc, o_win, z, gate_logits, x, w_out, ln_g, ln_b, alpha, tm=128):
    m, dm = x.shape
    width = N_HEADS * HEAD_DIM
    tm = min(tm, m)
    row = lambda i: (i, 0)
    fixed = lambda i: (0, 0)
    return pl.pallas_call(
        functools.partial(_nsa_post_kernel, alpha=alpha),
        grid=(m // tm,),
        in_specs=[pl.BlockSpec((tm, width), row),
                  pl.BlockSpec((tm, width), row),
                  pl.BlockSpec((tm, width), row),
                  pl.BlockSpec((tm, N_BRANCH * width), row),
                  pl.BlockSpec((tm, LANES), row),
                  pl.BlockSpec((tm, dm), row),
                  pl.BlockSpec((width, dm), fixed),
                  pl.BlockSpec((1, dm), fixed),
                  pl.BlockSpec((1, dm), fixed)],
        out_specs=[pl.BlockSpec((tm, dm), row), pl.BlockSpec((tm, dm), row)],
        out_shape=[jax.ShapeDtypeStruct((m, dm), F32), jax.ShapeDtypeStruct((m, dm), BF16)],
        compiler_params=_params("parallel"),
        name="nsa_post",
    )(o_cmp, o_slc, o_win, z, gate_logits, x, w_out, ln_g, ln_b)


def kernel(x, a_w_in, a_lam_re, a_lam_im, a_log_dt, a_b_re, a_b_im, a_c_re, a_c_im, a_d, a_w_glu, a_b_glu, a_w_out, kv_w, cmp_pos_k, cmp_w1_k, cmp_w2_k, cmp_pos_v, cmp_w1_v, cmp_w2_v, b_w_in, b_w_out, ln_g, ln_b):
    bsz, seq, dm = x.shape
    n_a = a_w_in.shape[0]
    n_b = b_w_in.shape[0]
    depth = n_a + n_b
    alpha = (2 * depth) ** 0.25
    width = N_HEADS * HEAD_DIM
    kvw = N_KV_HEADS * HEAD_DIM
    s_len = min(SCAN_T, seq) // SUBLANES

    xf = x.reshape(bsz * seq, dm).astype(F32)
    xb = xf.astype(BF16)

    for i in range(n_a):
        wb, cb, acoef = _s5_constants(a_lam_re[i], a_lam_im[i], a_log_dt[i], a_b_re[i], a_b_im[i],
                                      a_c_re[i], a_c_im[i], s_len)
        e = a_w_glu.shape[1]
        uz = _matmul(xb, a_w_in[i].astype(BF16), F32, lane_blocked=True)
        yssm = _s5_scan(uz, wb, cb, acoef, bsz, seq)
        xf, xb = _s5_post(yssm, uz, xf, a_d[i].reshape(1, e).astype(F32), a_w_glu[i].astype(BF16),
                          a_b_glu[i].reshape(1, e).astype(F32), a_w_out[i].astype(BF16),
                          ln_g[i].reshape(1, dm).astype(F32), ln_b[i].reshape(1, dm).astype(F32), alpha)

    kvc = _matmul(xb, kv_w[:, :2 * kvw].astype(BF16), F32)
    kvs = _matmul(xb, kv_w[:, 2 * kvw:].astype(BF16), BF16)
    cmp_kv = _compress(kvc,
                       jnp.stack([cmp_pos_k, cmp_pos_v]).astype(F32),
                       jnp.stack([cmp_w1_k, cmp_w1_v]).astype(BF16),
                       jnp.stack([cmp_w2_k, cmp_w2_v]).astype(BF16), bsz, seq)

    for i in range(n_b):
        layer = n_a + i
        w_in = b_w_in[i]
        n_gate = w_in.shape[1] - (1 + N_BRANCH) * width
        w_gate = jnp.pad(w_in[:, (1 + N_BRANCH) * width:], ((0, 0), (0, LANES - n_gate)))
        q = _matmul(xb, w_in[:, :width].astype(BF16), BF16, scale=HEAD_DIM ** -0.5)
        z = _matmul(xb, w_in[:, width:(1 + N_BRANCH) * width].astype(BF16), F32)
        gate_logits = _matmul(xb, w_gate.astype(BF16), F32)
        o_cmp, sel = _cmp_select(q, cmp_kv, bsz, seq)
        o_slc, o_win = _slc_win(q, sel, kvs, bsz, seq)
        xf, xb = _nsa_post(o_cmp, o_slc, o_win, z, gate_logits, xf, b_w_out[i].astype(BF16),
                           ln_g[layer].reshape(1, dm).astype(F32), ln_b[layer].reshape(1, dm).astype(F32), alpha)

    return xf.reshape(bsz, seq, dm).astype(x.dtype)
```

```python
import functools
import math

import jax
import jax.numpy as jnp
from jax import lax
from jax.experimental import pallas as pl
from jax.experimental.pallas import tpu as pltpu

F32 = jnp.float32
BF16 = jnp.bfloat16

HEAD_DIM = 128
N_HEADS = 16
N_KV_HEADS = 4
HEADS_PER_KV = N_HEADS // N_KV_HEADS
GROUP_WIDTH = HEADS_PER_KV * HEAD_DIM
N_BRANCH = 3
CMP_BLOCK = 32
CMP_STRIDE = 16
SEL_BLOCK = 64
N_SELECT = 16
WINDOW = 512
SSM_GROUP = 16
SSM_STATE = 64
SSM_CHUNK_GROUPS = 16
SSM_CHUNK_IN = SSM_CHUNK_GROUPS * SSM_GROUP
SSM_CHUNK_STATE = SSM_CHUNK_GROUPS * SSM_STATE
LN_EPS = 1e-5
MASK_VALUE = -1e30
FORCE_SCORE = 1e6
PAD_SCORE = -3e38
LOG2_E = math.log2(math.e)

SUBLANES = 8
LANES = 128
VMEM_LIMIT_BYTES = 48 * 1024 * 1024

SCAN_T = 256
CMP_TQ = 256
ATT_TQ = 128
ATT_TK = 512


def _params(*sem):
    return pltpu.CompilerParams(dimension_semantics=sem, vmem_limit_bytes=VMEM_LIMIT_BYTES)


def _sigmoid(v):
    return 1.0 / (1.0 + jnp.exp(-v))


def _layer_norm(v, g, b):
    mu = jnp.mean(v, axis=-1, keepdims=True)
    d = v - mu
    var = jnp.mean(d * d, axis=-1, keepdims=True)
    return d * lax.rsqrt(var + LN_EPS) * g + b


def _mm_kernel(a_ref, w_ref, o_ref, wb_ref, *, scale, lane_blocked):
    @pl.when(pl.program_id(1) == 0)
    def _():
        wb_ref[...] = w_ref[...].astype(BF16)

    acc = jnp.dot(a_ref[...], wb_ref[...], preferred_element_type=F32)
    if scale is not None:
        acc = acc * scale
    if lane_blocked:
        for j in range(o_ref.shape[0]):
            o_ref[j] = acc[:, j * LANES:(j + 1) * LANES].astype(o_ref.dtype)
    else:
        o_ref[...] = acc.astype(o_ref.dtype)


def _matmul(a, w, out_dtype, *, n=None, col0=0, layer=None, tm=512, tn=1024, scale=None, lane_blocked=False):
    m, k = a.shape
    n = w.shape[-1] if n is None else n
    tm = min(tm, m)
    tn = min(tn, n)
    assert m % tm == 0 and n % tn == 0 and col0 % tn == 0
    jb = col0 // tn
    if layer is None:
        w_spec = pl.BlockSpec((k, tn), lambda j, i: (0, j + jb))
    else:
        w_spec = pl.BlockSpec((None, k, tn), lambda j, i: (layer, 0, j + jb))
    if lane_blocked:
        out_spec = pl.BlockSpec((tn // LANES, tm, LANES), lambda j, i: (j, i, 0))
        out_shape = jax.ShapeDtypeStruct((n // LANES, m, LANES), out_dtype)
    else:
        out_spec = pl.BlockSpec((tm, tn), lambda j, i: (i, j))
        out_shape = jax.ShapeDtypeStruct((m, n), out_dtype)
    return pl.pallas_call(
        functools.partial(_mm_kernel, scale=scale, lane_blocked=lane_blocked),
        grid=(n // tn, m // tm),
        in_specs=[pl.BlockSpec((tm, k), lambda j, i: (i, 0)), w_spec],
        out_specs=out_spec,
        out_shape=out_shape,
        scratch_shapes=[pltpu.VMEM((k, tn), BF16)],
        compiler_params=_params("parallel", "arbitrary"),
        name="matmul",
    )(a, w)


def _s5_scan_kernel(u_ref, wb_ref, cb_ref, a_ref, o_ref, bu_ref, hb_ref, carry_ref, *, t_tok):
    s_len = t_tok // SUBLANES
    p = SSM_CHUNK_STATE

    @pl.when(pl.program_id(2) == 0)
    def _():
        carry_ref[...] = jnp.zeros_like(carry_ref)
        hb_ref[...] = jnp.zeros_like(hb_ref)

    n_lb = SSM_CHUNK_IN // LANES
    rows = [jnp.concatenate([u_ref[c, pl.ds(k, SUBLANES, stride=s_len), :] for c in range(n_lb)], axis=1)
            for k in range(s_len)]
    up = jnp.concatenate(rows, axis=0).astype(BF16)
    bu_ref[...] = jnp.dot(up, wb_ref[...], preferred_element_type=F32)

    y = jnp.dot(hb_ref[...].astype(BF16), cb_ref[...], preferred_element_type=F32)
    for k in range(s_len):
        for c in range(n_lb):
            o_ref[c, pl.ds(k, SUBLANES, stride=s_len), :] = y[k * SUBLANES:(k + 1) * SUBLANES,
                                                               c * LANES:(c + 1) * LANES]

    ar = jnp.broadcast_to(a_ref[0:1, :], (SUBLANES, p))
    ai = jnp.broadcast_to(a_ref[1:2, :], (SUBLANES, p))

    def step(k, hr, hi):
        br = bu_ref[k * SUBLANES:(k + 1) * SUBLANES, 0:p]
        bi = bu_ref[k * SUBLANES:(k + 1) * SUBLANES, p:2 * p]
        return ar * hr - ai * hi + br, ar * hi + ai * hr + bi

    er = jnp.zeros((SUBLANES, p), F32)
    ei = jnp.zeros((SUBLANES, p), F32)
    for k in range(s_len):
        er, ei = step(k, er, ei)

    sr = jnp.broadcast_to(a_ref[2:3, :], (SUBLANES, p))
    si = jnp.broadcast_to(a_ref[3:4, :], (SUBLANES, p))
    sub = lax.broadcasted_iota(jnp.int32, (SUBLANES, p), 0)
    hinr = jnp.where(sub == 0, carry_ref[:, 0:p], 0.0)
    hini = jnp.where(sub == 0, carry_ref[:, p:2 * p], 0.0)
    for j in range(SUBLANES - 1):
        nr = sr * hinr - si * hini + er
        ni = sr * hini + si * hinr + ei
        hinr = jnp.where(sub == j + 1, pltpu.roll(nr, 1, 0), hinr)
        hini = jnp.where(sub == j + 1, pltpu.roll(ni, 1, 0), hini)
    nr = sr * hinr - si * hini + er
    ni = sr * hini + si * hinr + ei
    carry_ref[:, 0:p] = jnp.broadcast_to(nr[SUBLANES - 1:SUBLANES, :], (SUBLANES, p))
    carry_ref[:, p:2 * p] = jnp.broadcast_to(ni[SUBLANES - 1:SUBLANES, :], (SUBLANES, p))

    hr, hi = hinr, hini
    for k in range(s_len):
        hr, hi = step(k, hr, hi)
        hb_ref[k * SUBLANES:(k + 1) * SUBLANES, 0:p] = hr
        hb_ref[k * SUBLANES:(k + 1) * SUBLANES, p:2 * p] = hi


def _s5_scan(uz, wb, cb, acoef, bsz, seq):
    n_chunks = wb.shape[0]
    e = n_chunks * SSM_CHUNK_IN
    n_lb = SSM_CHUNK_IN // LANES
    t_tok = min(SCAN_T, seq)
    nt = seq // t_tok
    assert seq % t_tok == 0 and t_tok % (SUBLANES * SUBLANES) == 0
    return pl.pallas_call(
        functools.partial(_s5_scan_kernel, t_tok=t_tok),
        grid=(bsz, n_chunks, nt + 1),
        in_specs=[
            pl.BlockSpec((n_lb, t_tok, LANES), lambda b, c, t: (c, b * nt + jnp.minimum(t, nt - 1), 0)),
            pl.BlockSpec((None, SSM_CHUNK_IN, 2 * SSM_CHUNK_STATE), lambda b, c, t: (c, 0, 0)),
            pl.BlockSpec((None, 2 * SSM_CHUNK_STATE, SSM_CHUNK_IN), lambda b, c, t: (c, 0, 0)),
            pl.BlockSpec((None, SUBLANES, SSM_CHUNK_STATE), lambda b, c, t: (c, 0, 0)),
        ],
        out_specs=pl.BlockSpec((n_lb, t_tok, LANES), lambda b, c, t: (c, b * nt + jnp.maximum(t - 1, 0), 0)),
        out_shape=jax.ShapeDtypeStruct((e // LANES, bsz * seq, LANES), F32),
        scratch_shapes=[pltpu.VMEM((t_tok, 2 * SSM_CHUNK_STATE), F32),
                        pltpu.VMEM((t_tok, 2 * SSM_CHUNK_STATE), F32),
                        pltpu.VMEM((SUBLANES, 2 * SSM_CHUNK_STATE), F32)],
        compiler_params=_params("parallel", "parallel", "arbitrary"),
        name="s5_scan",
    )(uz, wb, cb, acoef)


def _s5_constants(lam_re, lam_im, log_dt, b_re, b_im, c_re, c_im, s_len):
    lr = lam_re.astype(F32)
    li = lam_im.astype(F32)
    dt = jnp.exp(log_dt.astype(F32))[:, None]
    mag = jnp.exp(lr * dt)
    ar = mag * jnp.cos(li * dt)
    ai = mag * jnp.sin(li * dt)
    inv_abs2 = 1.0 / (lr * lr + li * li)
    cr = ((ar - 1.0) * lr + ai * li) * inv_abs2
    ci = (ai * lr - (ar - 1.0) * li) * inv_abs2
    br = b_re.astype(F32)
    bi = b_im.astype(F32)
    bbar_r = cr[..., None] * br - ci[..., None] * bi
    bbar_i = cr[..., None] * bi + ci[..., None] * br
    g = lr.shape[0]
    nck = g // SSM_CHUNK_GROUPS
    eye = jnp.eye(SSM_CHUNK_GROUPS, dtype=F32)

    def in_blocks(w):
        w = w.reshape(nck, SSM_CHUNK_GROUPS, SSM_STATE, SSM_GROUP)
        return jnp.einsum('kgpc,gh->kgchp', w, eye).reshape(nck, SSM_CHUNK_IN, SSM_CHUNK_STATE)

    def out_blocks(w):
        w = w.reshape(nck, SSM_CHUNK_GROUPS, SSM_GROUP, SSM_STATE)
        return jnp.einsum('kgcp,gh->kgphc', w, eye).reshape(nck, SSM_CHUNK_STATE, SSM_CHUNK_IN)

    wb = jnp.concatenate([in_blocks(bbar_r), in_blocks(bbar_i)], axis=2).astype(BF16)
    cb = jnp.concatenate([out_blocks(c_re.astype(F32)), -out_blocks(c_im.astype(F32))], axis=1).astype(BF16)
    sr, si = ar, ai
    for _ in range(int(math.log2(s_len))):
        sr, si = sr * sr - si * si, 2.0 * sr * si
    rows = [v.reshape(nck, 1, SSM_CHUNK_STATE) for v in (ar, ai, sr, si)]
    rows += [jnp.zeros_like(rows[0])] * (SUBLANES - len(rows))
    return wb, cb, jnp.concatenate(rows, axis=1)


def _s5_post_kernel(y_ref, u_ref, z_ref, x_ref, d_ref, wg_ref, bg_ref, wo_ref, lg_ref, lb_ref,
                    xo_ref, xb_ref, *, alpha):
    unblock = lambda ref: jnp.concatenate([ref[j] for j in range(ref.shape[0])], axis=1)
    y = unblock(y_ref) + d_ref[...] * unblock(u_ref)
    g = jax.nn.gelu(y)
    gl = jnp.dot(g.astype(BF16), wg_ref[...], preferred_element_type=F32) + bg_ref[...]
    y2 = g * _sigmoid(gl)
    z = unblock(z_ref)
    a = (y2 * (z * _sigmoid(z))).astype(BF16)
    o = jnp.dot(a, wo_ref[...], preferred_element_type=F32)
    xn = _layer_norm(alpha * x_ref[...] + o, lg_ref[...], lb_ref[...])
    xo_ref[...] = xn
    xb_ref[...] = xn.astype(BF16)


def _s5_post(yssm, uz, x, d, w_glu, b_glu, w_out, ln_g, ln_b, alpha, tm=256):
    nlb, m, _ = yssm.shape
    e = nlb * LANES
    dm = x.shape[1]
    tm = min(tm, m)
    row = lambda i: (i, 0)
    fixed = lambda i: (0, 0)
    return pl.pallas_call(
        functools.partial(_s5_post_kernel, alpha=alpha),
        grid=(m // tm,),
        in_specs=[pl.BlockSpec((nlb, tm, LANES), lambda i: (0, i, 0)),
                  pl.BlockSpec((nlb, tm, LANES), lambda i: (0, i, 0)),
                  pl.BlockSpec((nlb, tm, LANES), lambda i: (1, i, 0)),
                  pl.BlockSpec((tm, dm), row),
                  pl.BlockSpec((1, e), fixed),
                  pl.BlockSpec((e, e), fixed),
                  pl.BlockSpec((1, e), fixed),
                  pl.BlockSpec((e, dm), fixed),
                  pl.BlockSpec((1, dm), fixed),
                  pl.BlockSpec((1, dm), fixed)],
        out_specs=[pl.BlockSpec((tm, dm), row), pl.BlockSpec((tm, dm), row)],
        out_shape=[jax.ShapeDtypeStruct((m, dm), F32), jax.ShapeDtypeStruct((m, dm), BF16)],
        compiler_params=_params("parallel"),
        name="s5_post",
    )(yssm, uz, uz, x, d, w_glu, b_glu, w_out, ln_g, ln_b)


def _compress_kernel(t_ref, pos_ref, w1_ref, w2_ref, o_ref, q_ref, *, nc):
    half = CMP_BLOCK // 2
    acc_p = jnp.zeros((nc, HEAD_DIM), F32)
    acc_q = jnp.zeros((nc, HEAD_DIM), F32)
    for l in range(half):
        t = t_ref[pl.ds(l, nc, stride=CMP_STRIDE), :]
        lo = (t + pos_ref[l:l + 1, :]).astype(BF16)
        hi = (t + pos_ref[half + l:half + l + 1, :]).astype(BF16)
        acc_p += jnp.dot(lo, w1_ref[l * HEAD_DIM:(l + 1) * HEAD_DIM, :], preferred_element_type=F32)
        acc_q += jnp.dot(hi, w1_ref[(half + l) * HEAD_DIM:(half + l + 1) * HEAD_DIM, :],
                         preferred_element_type=F32)
    q_ref[0:nc, :] = acc_q
    q_ref[nc:nc + SUBLANES, :] = jnp.zeros((SUBLANES, HEAD_DIM), F32)
    pre = acc_p + q_ref[pl.ds(1, nc), :]
    mid = jax.nn.gelu(pre).astype(BF16)
    o_ref[...] = jnp.dot(mid, w2_ref[...], preferred_element_type=F32).astype(o_ref.dtype)


def _compress(kvc, pos, w1, w2, bsz, seq):
    nc = seq // CMP_STRIDE
    g = N_KV_HEADS
    return pl.pallas_call(
        functools.partial(_compress_kernel, nc=nc),
        grid=(bsz, 2, g),
        in_specs=[pl.BlockSpec((seq, HEAD_DIM), lambda b, s, h: (b, s * g + h)),
                  pl.BlockSpec((None, CMP_BLOCK, HEAD_DIM), lambda b, s, h: (s, 0, 0)),
                  pl.BlockSpec((None, CMP_BLOCK * HEAD_DIM, HEAD_DIM), lambda b, s, h: (s, 0, 0)),
                  pl.BlockSpec((None, HEAD_DIM, HEAD_DIM), lambda b, s, h: (s, 0, 0))],
        out_specs=pl.BlockSpec((None, None, None, nc, HEAD_DIM), lambda b, s, h: (b, s, h, 0, 0)),
        out_shape=jax.ShapeDtypeStruct((bsz, 2, g, nc, HEAD_DIM), BF16),
        scratch_shapes=[pltpu.VMEM((nc + SUBLANES, HEAD_DIM), F32)],
        compiler_params=_params("parallel", "parallel", "parallel"),
        name="kv_compress",
    )(kvc, pos, w1, w2)


def _cmp_select_kernel(q_ref, k_ref, v_ref, o_ref, sel_ref, *, tq, nc, ns, n_sel):
    q0 = pl.program_id(2) * tq
    k = k_ref[...]
    v = v_ref[...]
    t = q0 + lax.broadcasted_iota(jnp.int32, (tq, nc), 0)
    n = lax.broadcasted_iota(jnp.int32, (tq, nc), 1)
    valid = n * CMP_STRIDE + (CMP_BLOCK - 1) <= t
    psum = jnp.zeros((tq, nc), F32)
    for h in range(HEADS_PER_KV):
        qh = q_ref[:, h * HEAD_DIM:(h + 1) * HEAD_DIM]
        s = lax.dot_general(qh, k, (((1,), (1,)), ((), ())), preferred_element_type=F32)
        s = jnp.where(valid, s, MASK_VALUE)
        m = jnp.max(s, axis=-1, keepdims=True)
        e = jnp.where(valid, jnp.exp(s - m), 0.0)
        p = e / jnp.maximum(jnp.sum(e, axis=-1, keepdims=True), 1e-30)
        o_ref[:, h * HEAD_DIM:(h + 1) * HEAD_DIM] = jnp.dot(p.astype(BF16), v, preferred_element_type=F32)
        psum = psum + p

    nn = lax.broadcasted_iota(jnp.int32, (nc, LANES), 0)
    jj = lax.broadcasted_iota(jnp.int32, (nc, LANES), 1)
    overlap = ((nn * CMP_STRIDE < (jj + 1) * SEL_BLOCK)
               & (nn * CMP_STRIDE + (CMP_BLOCK - 1) >= jj * SEL_BLOCK)
               & (jj < ns)).astype(BF16)
    p_hi = psum.astype(BF16)
    p_lo = (psum - p_hi.astype(F32)).astype(BF16)
    imp = (jnp.dot(p_hi, overlap, preferred_element_type=F32)
           + jnp.dot(p_lo, overlap, preferred_element_type=F32))

    tt = q0 + lax.broadcasted_iota(jnp.int32, (tq, LANES), 0)
    j = lax.broadcasted_iota(jnp.int32, (tq, LANES), 1)
    cur = tt // SEL_BLOCK
    forced = (j == 0) | (j == cur) | (j == cur - 1)
    imp = jnp.where(j > cur, MASK_VALUE, jnp.where(forced, FORCE_SCORE, imp))
    imp = jnp.where(j < ns, imp, PAD_SCORE)

    x = imp.T
    nblk = ns // SUBLANES
    xs = [x[b * SUBLANES:(b + 1) * SUBLANES, :] for b in range(nblk)]
    cnts = [jnp.zeros((SUBLANES, tq), jnp.int32) for _ in range(nblk)]
    sub = lax.broadcasted_iota(jnp.int32, (SUBLANES, tq), 0)
    for i in range(ns):
        xi = x[i:i + 1, :]
        for b in range(nblk):
            if b * SUBLANES > i:
                beats = xi >= xs[b]
            elif (b + 1) * SUBLANES - 1 <= i:
                beats = xi > xs[b]
            else:
                beats = (xi > xs[b]) | ((xi >= xs[b]) & (sub > i - b * SUBLANES))
            cnts[b] = cnts[b] + beats.astype(jnp.int32)
    sel_t = jnp.concatenate([(c < n_sel).astype(F32) for c in cnts]
                            + [jnp.zeros((LANES - ns, tq), F32)] * (ns < LANES), axis=0)
    sel_ref[...] = sel_t.T.astype(sel_ref.dtype)


def _cmp_select(q, cmp_kv, bsz, seq):
    tq = min(CMP_TQ, seq)
    nq = seq // tq
    nc = seq // CMP_STRIDE
    ns = seq // SEL_BLOCK
    assert ns <= LANES and ns % SUBLANES == 0 and tq % LANES == 0
    g = N_KV_HEADS
    return pl.pallas_call(
        functools.partial(_cmp_select_kernel, tq=tq, nc=nc, ns=ns, n_sel=min(N_SELECT, ns)),
        grid=(bsz, g, nq),
        in_specs=[pl.BlockSpec((tq, GROUP_WIDTH), lambda b, h, i: (b * nq + i, h)),
                  pl.BlockSpec((None, None, None, nc, HEAD_DIM), lambda b, h, i: (b, 0, h, 0, 0)),
                  pl.BlockSpec((None, None, None, nc, HEAD_DIM), lambda b, h, i: (b, 1, h, 0, 0))],
        out_specs=[pl.BlockSpec((tq, GROUP_WIDTH), lambda b, h, i: (b * nq + i, h)),
                   pl.BlockSpec((None, None, tq, LANES), lambda b, h, i: (b, h, i, 0))],
        out_shape=[jax.ShapeDtypeStruct((bsz * seq, N_HEADS * HEAD_DIM), F32),
                   jax.ShapeDtypeStruct((bsz, g, seq, LANES), BF16)],
        compiler_params=_params("parallel", "parallel", "parallel"),
        name="nsa_cmp_select",
    )(q, cmp_kv, cmp_kv)


def _lane_chunks(v):
    return [v[:, c * LANES:(c + 1) * LANES] for c in range(v.shape[1] // LANES)]


def _slc_win_kernel(q_ref, sel_ref, oh_ref, ks_ref, vs_ref, kw_ref, vw_ref, os_ref, ow_ref,
                    sa_ref, sb_ref, ma_ref, mb_ref, *, tq, tk, seq):
    step = pl.program_id(2)
    nq = pl.num_programs(2) - 1

    @pl.when(step % 2 == 0)
    def _():
        _slc_win_step(q_ref, sel_ref, oh_ref, ks_ref, vs_ref, kw_ref, vw_ref, os_ref, ow_ref,
                      sa_ref, ma_ref, sb_ref, mb_ref, step, nq, tq=tq, tk=tk, seq=seq)

    @pl.when(step % 2 == 1)
    def _():
        _slc_win_step(q_ref, sel_ref, oh_ref, ks_ref, vs_ref, kw_ref, vw_ref, os_ref, ow_ref,
                      sb_ref, mb_ref, sa_ref, ma_ref, step, nq, tq=tq, tk=tk, seq=seq)


def _slc_win_step(q_ref, sel_ref, oh_ref, ks_ref, vs_ref, kw_ref, vw_ref, os_ref, ow_ref,
                  s_ref, m_ref, sp_ref, mp_ref, step, nq, *, tq, tk, seq):
    hp = HEADS_PER_KV
    rows = hp * tq
    cur = jnp.minimum(step, nq - 1)
    q0 = cur * tq
    q4 = jnp.concatenate([q_ref[:, h * HEAD_DIM:(h + 1) * HEAD_DIM] for h in range(hp)], axis=0)
    neg = ((sel_ref[...].astype(F32) - 1.0) * (-MASK_VALUE)).astype(BF16)
    q_aug = jnp.concatenate([q4, jnp.concatenate([neg] * hp, axis=0)], axis=1)

    n_tiles = (q0 + tq - 1) // tk + 1
    last = n_tiles - 1
    n_prev = jnp.where(step == 0, 0, ((step - 1) * tq + tq - 1) // tk + 1)
    nt_dims = (((1,), (1,)), ((), ()))

    def masked_scores(kt):
        k0 = pl.multiple_of(kt * tk, tk)
        k_aug = jnp.concatenate([ks_ref[pl.ds(k0, tk), :], oh_ref[pl.ds(k0, tk), :]], axis=1)
        s = lax.dot_general(q_aug, k_aug, nt_dims, preferred_element_type=F32)
        return s * LOG2_E

    def lane_max(s, mx):
        for sc in _lane_chunks(s):
            mx = jnp.maximum(mx, sc)
        return mx

    def weighted(kt, ls, acc):
        k0 = pl.multiple_of(kt * tk, tk)
        mp = mp_ref[...]
        es = [jnp.exp2(sc - mp) for sc in _lane_chunks(sp_ref[kt])]
        for e in es:
            ls = ls + e
        e = jnp.concatenate([e.astype(BF16) for e in es], axis=1)
        return ls, acc + jnp.dot(e, vs_ref[pl.ds(k0, tk), :], preferred_element_type=F32)

    def both(kt, carry):
        mx, ls, acc = carry
        s = masked_scores(kt)
        s_ref[kt] = s
        ls, acc = weighted(kt, ls, acc)
        return lane_max(s, mx), ls, acc

    carry = (jnp.full((rows, LANES), MASK_VALUE, F32), jnp.zeros((rows, LANES), F32),
             jnp.zeros((rows, HEAD_DIM), F32))
    n_pairs = last // 2
    carry = lax.fori_loop(0, n_pairs, lambda i, c: both(2 * i + 1, both(2 * i, c)), carry)
    mx, ls, acc = lax.fori_loop(2 * n_pairs, last, both, carry)

    t = q0 + lax.broadcasted_iota(jnp.int32, (tq, tk), 0)
    kcol = last * tk + lax.broadcasted_iota(jnp.int32, (tq, tk), 1)
    s = jnp.where((kcol <= t)[None], masked_scores(last).reshape(hp, tq, tk), MASK_VALUE).reshape(rows, tk)
    s_ref[last] = s
    mx = lane_max(s, mx)
    m_ref[...] = jnp.broadcast_to(jnp.max(mx, axis=-1, keepdims=True), (rows, LANES))

    span = min(WINDOW + tq, seq)
    start = pl.multiple_of(jnp.clip(q0 + tq - span, 0, seq - span), tq)
    kpos = start + lax.broadcasted_iota(jnp.int32, (tq, span), 1)
    tw = q0 + lax.broadcasted_iota(jnp.int32, (tq, span), 0)
    wbias = jnp.where((kpos <= tw) & (kpos > tw - WINDOW), 0.0, MASK_VALUE)
    sw = lax.dot_general(q4, kw_ref[pl.ds(start, span), :], nt_dims, preferred_element_type=F32)
    sw = (sw.reshape(hp, tq, span) + wbias[None]).reshape(rows, span)
    chunks = _lane_chunks(sw)
    mw = chunks[0]
    for sc in chunks[1:]:
        mw = jnp.maximum(mw, sc)
    mw = jnp.broadcast_to(jnp.max(mw, axis=-1, keepdims=True), (rows, LANES))
    es = [jnp.exp(sc - mw) for sc in chunks]
    lw = es[0]
    for e in es[1:]:
        lw = lw + e
    e = jnp.concatenate([e.astype(BF16) for e in es], axis=1)
    ow = jnp.dot(e, vw_ref[pl.ds(start, span), :], preferred_element_type=F32)
    ow = ow / jnp.sum(lw, axis=-1, keepdims=True)
    for h in range(hp):
        ow_ref[:, h * HEAD_DIM:(h + 1) * HEAD_DIM] = ow[h * tq:(h + 1) * tq, :]

    ls, acc = lax.fori_loop(last, n_prev, lambda kt, c: weighted(kt, *c), (ls, acc))
    o = acc / jnp.maximum(jnp.sum(ls, axis=-1, keepdims=True), 1e-30)
    for h in range(hp):
        os_ref[:, h * HEAD_DIM:(h + 1) * HEAD_DIM] = o[h * tq:(h + 1) * tq, :]


def _slc_win(q, sel, kvs, bsz, seq):
    tq = min(ATT_TQ, seq)
    tk = min(ATT_TK, seq)
    nq = seq // tq
    g = N_KV_HEADS
    block_onehot = (jnp.arange(seq)[:, None] // SEL_BLOCK == jnp.arange(LANES)[None, :]).astype(BF16)
    kv_spec = lambda part: pl.BlockSpec((seq, HEAD_DIM), lambda b, h, i: (b, part * g + h))
    cur = lambda i: jnp.minimum(i, nq - 1)
    prev = lambda i: jnp.maximum(i - 1, 0)
    out_sds = jax.ShapeDtypeStruct((bsz * seq, N_HEADS * HEAD_DIM), F32)
    rows = HEADS_PER_KV * tq
    return pl.pallas_call(
        functools.partial(_slc_win_kernel, tq=tq, tk=tk, seq=seq),
        grid=(bsz, g, nq + 1),
        in_specs=[pl.BlockSpec((tq, GROUP_WIDTH), lambda b, h, i: (b * nq + cur(i), h)),
                  pl.BlockSpec((None, None, tq, LANES), lambda b, h, i: (b, h, cur(i), 0)),
                  pl.BlockSpec((seq, LANES), lambda b, h, i: (0, 0)),
                  kv_spec(0), kv_spec(1), kv_spec(2), kv_spec(3)],
        out_specs=[pl.BlockSpec((tq, GROUP_WIDTH), lambda b, h, i: (b * nq + prev(i), h)),
                   pl.BlockSpec((tq, GROUP_WIDTH), lambda b, h, i: (b * nq + cur(i), h))],
        out_shape=[out_sds, out_sds],
        scratch_shapes=[pltpu.VMEM((seq // tk, rows, tk), F32),
                        pltpu.VMEM((seq // tk, rows, tk), F32),
                        pltpu.VMEM((rows, LANES), F32),
                        pltpu.VMEM((rows, LANES), F32)],
        compiler_params=_params("parallel", "parallel", "arbitrary"),
        name="nsa_slc_win",
    )(q, sel, block_onehot, kvs, kvs, kvs, kvs)


def _nsa_post_kernel(oc_ref, os_ref, ow_ref, z_ref, gl_ref, x_ref, wo_ref, lg_ref, lb_ref,
                     xo_ref, xb_ref, *, alpha):
    width = N_HEADS * HEAD_DIM
    gate = _sigmoid(gl_ref[...])
    parts = []
    for h in range(N_HEADS):
        hs = slice(h * HEAD_DIM, (h + 1) * HEAD_DIM)
        acc = None
        for br, o_ref in enumerate((oc_ref, os_ref, ow_ref)):
            gcol = gate[:, br * N_HEADS + h:br * N_HEADS + h + 1]
            z = z_ref[:, br * width + h * HEAD_DIM:br * width + (h + 1) * HEAD_DIM]
            term = gcol * o_ref[:, hs] * (z * _sigmoid(z))
            acc = term if acc is None else acc + term
        parts.append(acc.astype(BF16))
    y = jnp.concatenate(parts, axis=1)
    o = jnp.dot(y, wo_ref[...], preferred_element_type=F32)
    xn = _layer_norm(alpha * x_ref[...] + o, lg_ref[...], lb_ref[...])
    xo_ref[...] = xn
    xb_ref[...] = xn.astype(BF16)


def _nsa_post(o_cmp, o_slc, o_win, z, gate_logits, x, w_out, ln_g, ln_b, alpha, tm=128):
    m, dm = x.shape
    width = N_HEADS * HEAD_DIM
    tm = min(tm, m)
    row = lambda i: (i, 0)
    fixed = lambda i: (0, 0)
    return pl.pallas_call(
        functools.partial(_nsa_post_kernel, alpha=alpha),
        grid=(m // tm,),
        in_specs=[pl.BlockSpec((tm, width), row),
                  pl.BlockSpec((tm, width), row),
                  pl.BlockSpec((tm, width), row),
                  pl.BlockSpec((tm, N_BRANCH * width), row),
                  pl.BlockSpec((tm, LANES), row),
                  pl.BlockSpec((tm, dm), row),
                  pl.BlockSpec((width, dm), fixed),
                  pl.BlockSpec((1, dm), fixed),
                  pl.BlockSpec((1, dm), fixed)],
        out_specs=[pl.BlockSpec((tm, dm), row), pl.BlockSpec((tm, dm), row)],
        out_shape=[jax.ShapeDtypeStruct((m, dm), F32), jax.ShapeDtypeStruct((m, dm), BF16)],
        compiler_params=_params("parallel"),
        name="nsa_post",
    )(o_cmp, o_slc, o_win, z, gate_logits, x, w_out, ln_g, ln_b)


def kernel(x, a_w_in, a_lam_re, a_lam_im, a_log_dt, a_b_re, a_b_im, a_c_re, a_c_im, a_d, a_w_glu, a_b_glu, a_w_out, kv_w, cmp_pos_k, cmp_w1_k, cmp_w2_k, cmp_pos_v, cmp_w1_v, cmp_w2_v, b_w_in, b_w_out, ln_g, ln_b):
    bsz, seq, dm = x.shape
    n_a = a_w_in.shape[0]
    n_b = b_w_in.shape[0]
    depth = n_a + n_b
    alpha = (2 * depth) ** 0.25
    width = N_HEADS * HEAD_DIM
    kvw = N_KV_HEADS * HEAD_DIM
    s_len = min(SCAN_T, seq) // SUBLANES

    xf = x.reshape(bsz * seq, dm).astype(F32)
    xb = xf.astype(BF16)

    for i in range(n_a):
        wb, cb, acoef = _s5_constants(a_lam_re[i], a_lam_im[i], a_log_dt[i], a_b_re[i], a_b_im[i],
                                      a_c_re[i], a_c_im[i], s_len)
        e = a_w_glu.shape[1]
        uz = _matmul(xb, a_w_in.astype(F32), F32, layer=i, lane_blocked=True)
        yssm = _s5_scan(uz, wb, cb, acoef, bsz, seq)
        xf, xb = _s5_post(yssm, uz, xf, a_d[i].reshape(1, e).astype(F32), a_w_glu[i].astype(BF16),
                          a_b_glu[i].reshape(1, e).astype(F32), a_w_out[i].astype(BF16),
                          ln_g[i].reshape(1, dm).astype(F32), ln_b[i].reshape(1, dm).astype(F32), alpha)

    kvc = _matmul(xb, kv_w.astype(F32), F32, n=2 * kvw)
    kvs = _matmul(xb, kv_w.astype(F32), BF16, col0=2 * kvw, n=4 * kvw)
    cmp_kv = _compress(kvc,
                       jnp.stack([cmp_pos_k, cmp_pos_v]).astype(F32),
                       jnp.stack([cmp_w1_k, cmp_w1_v]).astype(BF16),
                       jnp.stack([cmp_w2_k, cmp_w2_v]).astype(BF16), bsz, seq)

    for i in range(n_b):
        layer = n_a + i
        w_in = b_w_in[i]
        n_gate = w_in.shape[1] - (1 + N_BRANCH) * width
        w_gate = jnp.pad(w_in[:, (1 + N_BRANCH) * width:], ((0, 0), (0, LANES - n_gate)))
        q = _matmul(xb, b_w_in.astype(F32), BF16, layer=i, n=width, scale=HEAD_DIM ** -0.5)
        z = _matmul(xb, b_w_in.astype(F32), F32, layer=i, col0=width, n=N_BRANCH * width)
        gate_logits = _matmul(xb, w_gate.astype(F32), F32)
        o_cmp, sel = _cmp_select(q, cmp_kv, bsz, seq)
        o_slc, o_win = _slc_win(q, sel, kvs, bsz, seq)
        xf, xb = _nsa_post(o_cmp, o_slc, o_win, z, gate_logits, xf, b_w_out[i].astype(BF16),
                           ln_g[layer].reshape(1, dm).astype(F32), ln_b[layer].reshape(1, dm).astype(F32), alpha)

    return xf.reshape(bsz, seq, dm).astype(x.dtype)
```

```python
import functools
import math

import jax
import jax.numpy as jnp
from jax import lax
from jax.experimental import pallas as pl
from jax.experimental.pallas import tpu as pltpu

F32 = jnp.float32
BF16 = jnp.bfloat16

HEAD_DIM = 128
N_HEADS = 16
N_KV_HEADS = 4
HEADS_PER_KV = N_HEADS // N_KV_HEADS
GROUP_WIDTH = HEADS_PER_KV * HEAD_DIM
N_BRANCH = 3
CMP_BLOCK = 32
CMP_STRIDE = 16
SEL_BLOCK = 64
N_SELECT = 16
WINDOW = 512
SSM_GROUP = 16
SSM_STATE = 64
SSM_CHUNK_GROUPS = 16
SSM_CHUNK_IN = SSM_CHUNK_GROUPS * SSM_GROUP
SSM_CHUNK_STATE = SSM_CHUNK_GROUPS * SSM_STATE
LN_EPS = 1e-5
MASK_VALUE = -1e30
FORCE_SCORE = 1e6
PAD_SCORE = -3e38
LOG2_E = math.log2(math.e)

SUBLANES = 8
LANES = 128
VMEM_LIMIT_BYTES = 48 * 1024 * 1024

SCAN_T = 256
CMP_TQ = 256
ATT_TQ = 512
ATT_TK = 512
WIN_TQ = 128
ATT_VMEM_LIMIT_BYTES = 58 * 1024 * 1024


def _params(*sem):
    return pltpu.CompilerParams(dimension_semantics=sem, vmem_limit_bytes=VMEM_LIMIT_BYTES)


def _sigmoid(v):
    return 1.0 / (1.0 + jnp.exp(-v))


def _layer_norm(v, g, b):
    mu = jnp.mean(v, axis=-1, keepdims=True)
    d = v - mu
    var = jnp.mean(d * d, axis=-1, keepdims=True)
    return d * lax.rsqrt(var + LN_EPS) * g + b


def _mm_kernel(a_ref, w_ref, o_ref, *wb_refs, scale, lane_blocked):
    if wb_refs:
        wb_ref, = wb_refs

        @pl.when(pl.program_id(1) == 0)
        def _():
            wb_ref[...] = w_ref[...].astype(BF16)

        w_ref = wb_ref
    acc = jnp.dot(a_ref[...], w_ref[...], preferred_element_type=F32)
    if scale is not None:
        acc = acc * scale
    if lane_blocked:
        for j in range(o_ref.shape[0]):
            o_ref[j] = acc[:, j * LANES:(j + 1) * LANES].astype(o_ref.dtype)
    else:
        o_ref[...] = acc.astype(o_ref.dtype)


def _matmul(a, w, out_dtype, *, n=None, col0=0, layer=None, tm=512, tn=1024, scale=None, lane_blocked=False):
    m, k = a.shape
    n = w.shape[-1] if n is None else n
    tm = min(tm, m)
    tn = min(tn, n)
    assert m % tm == 0 and n % tn == 0 and col0 % tn == 0
    jb = col0 // tn
    if layer is None:
        w_spec = pl.BlockSpec((k, tn), lambda j, i: (0, j + jb))
    else:
        w_spec = pl.BlockSpec((None, k, tn), lambda j, i: (layer, 0, j + jb))
    if lane_blocked:
        out_spec = pl.BlockSpec((tn // LANES, tm, LANES), lambda j, i: (j, i, 0))
        out_shape = jax.ShapeDtypeStruct((n // LANES, m, LANES), out_dtype)
    else:
        out_spec = pl.BlockSpec((tm, tn), lambda j, i: (i, j))
        out_shape = jax.ShapeDtypeStruct((m, n), out_dtype)
    return pl.pallas_call(
        functools.partial(_mm_kernel, scale=scale, lane_blocked=lane_blocked),
        grid=(n // tn, m // tm),
        in_specs=[pl.BlockSpec((tm, k), lambda j, i: (i, 0)), w_spec],
        out_specs=out_spec,
        out_shape=out_shape,
        scratch_shapes=[] if w.dtype == BF16 else [pltpu.VMEM((k, tn), BF16)],
        compiler_params=_params("parallel", "arbitrary"),
        name="matmul",
    )(a, w)


def _s5_scan_kernel(u_ref, wb_ref, cb_ref, a_ref, o_ref, bu_ref, hb_ref, carry_ref, *, t_tok):
    s_len = t_tok // SUBLANES
    p = SSM_CHUNK_STATE

    @pl.when(pl.program_id(2) == 0)
    def _():
        carry_ref[...] = jnp.zeros_like(carry_ref)
        hb_ref[...] = jnp.zeros_like(hb_ref)

    n_lb = SSM_CHUNK_IN // LANES
    rows = [jnp.concatenate([u_ref[c, pl.ds(k, SUBLANES, stride=s_len), :] for c in range(n_lb)], axis=1)
            for k in range(s_len)]
    up = jnp.concatenate(rows, axis=0).astype(BF16)
    bu_ref[...] = jnp.dot(up, wb_ref[...], preferred_element_type=F32)

    y = jnp.dot(hb_ref[...].astype(BF16), cb_ref[...], preferred_element_type=F32)
    for k in range(s_len):
        for c in range(n_lb):
            o_ref[c, pl.ds(k, SUBLANES, stride=s_len), :] = y[k * SUBLANES:(k + 1) * SUBLANES,
                                                               c * LANES:(c + 1) * LANES]

    ar = jnp.broadcast_to(a_ref[0:1, :], (SUBLANES, p))
    ai = jnp.broadcast_to(a_ref[1:2, :], (SUBLANES, p))

    def step(k, hr, hi):
        br = bu_ref[k * SUBLANES:(k + 1) * SUBLANES, 0:p]
        bi = bu_ref[k * SUBLANES:(k + 1) * SUBLANES, p:2 * p]
        return ar * hr - ai * hi + br, ar * hi + ai * hr + bi

    er = jnp.zeros((SUBLANES, p), F32)
    ei = jnp.zeros((SUBLANES, p), F32)
    for k in range(s_len):
        er, ei = step(k, er, ei)

    sr = jnp.broadcast_to(a_ref[2:3, :], (SUBLANES, p))
    si = jnp.broadcast_to(a_ref[3:4, :], (SUBLANES, p))
    sub = lax.broadcasted_iota(jnp.int32, (SUBLANES, p), 0)
    hinr = jnp.where(sub == 0, carry_ref[:, 0:p], 0.0)
    hini = jnp.where(sub == 0, carry_ref[:, p:2 * p], 0.0)
    for j in range(SUBLANES - 1):
        nr = sr * hinr - si * hini + er
        ni = sr * hini + si * hinr + ei
        hinr = jnp.where(sub == j + 1, pltpu.roll(nr, 1, 0), hinr)
        hini = jnp.where(sub == j + 1, pltpu.roll(ni, 1, 0), hini)
    nr = sr * hinr - si * hini + er
    ni = sr * hini + si * hinr + ei
    carry_ref[:, 0:p] = jnp.broadcast_to(nr[SUBLANES - 1:SUBLANES, :], (SUBLANES, p))
    carry_ref[:, p:2 * p] = jnp.broadcast_to(ni[SUBLANES - 1:SUBLANES, :], (SUBLANES, p))

    hr, hi = hinr, hini
    for k in range(s_len):
        hr, hi = step(k, hr, hi)
        hb_ref[k * SUBLANES:(k + 1) * SUBLANES, 0:p] = hr
        hb_ref[k * SUBLANES:(k + 1) * SUBLANES, p:2 * p] = hi


def _s5_scan(uz, wb, cb, acoef, bsz, seq):
    n_chunks = wb.shape[0]
    e = n_chunks * SSM_CHUNK_IN
    n_lb = SSM_CHUNK_IN // LANES
    t_tok = min(SCAN_T, seq)
    nt = seq // t_tok
    assert seq % t_tok == 0 and t_tok % (SUBLANES * SUBLANES) == 0
    return pl.pallas_call(
        functools.partial(_s5_scan_kernel, t_tok=t_tok),
        grid=(bsz, n_chunks, nt + 1),
        in_specs=[
            pl.BlockSpec((n_lb, t_tok, LANES), lambda b, c, t: (c, b * nt + jnp.minimum(t, nt - 1), 0)),
            pl.BlockSpec((None, SSM_CHUNK_IN, 2 * SSM_CHUNK_STATE), lambda b, c, t: (c, 0, 0)),
            pl.BlockSpec((None, 2 * SSM_CHUNK_STATE, SSM_CHUNK_IN), lambda b, c, t: (c, 0, 0)),
            pl.BlockSpec((None, SUBLANES, SSM_CHUNK_STATE), lambda b, c, t: (c, 0, 0)),
        ],
        out_specs=pl.BlockSpec((n_lb, t_tok, LANES), lambda b, c, t: (c, b * nt + jnp.maximum(t - 1, 0), 0)),
        out_shape=jax.ShapeDtypeStruct((e // LANES, bsz * seq, LANES), F32),
        scratch_shapes=[pltpu.VMEM((t_tok, 2 * SSM_CHUNK_STATE), F32),
                        pltpu.VMEM((t_tok, 2 * SSM_CHUNK_STATE), F32),
                        pltpu.VMEM((SUBLANES, 2 * SSM_CHUNK_STATE), F32)],
        compiler_params=_params("parallel", "parallel", "arbitrary"),
        name="s5_scan",
    )(uz, wb, cb, acoef)


def _s5_constants(lam_re, lam_im, log_dt, b_re, b_im, c_re, c_im, s_len):
    lr = lam_re.astype(F32)
    li = lam_im.astype(F32)
    dt = jnp.exp(log_dt.astype(F32))[:, None]
    mag = jnp.exp(lr * dt)
    ar = mag * jnp.cos(li * dt)
    ai = mag * jnp.sin(li * dt)
    inv_abs2 = 1.0 / (lr * lr + li * li)
    cr = ((ar - 1.0) * lr + ai * li) * inv_abs2
    ci = (ai * lr - (ar - 1.0) * li) * inv_abs2
    br = b_re.astype(F32)
    bi = b_im.astype(F32)
    bbar_r = cr[..., None] * br - ci[..., None] * bi
    bbar_i = cr[..., None] * bi + ci[..., None] * br
    g = lr.shape[0]
    nck = g // SSM_CHUNK_GROUPS
    eye = jnp.eye(SSM_CHUNK_GROUPS, dtype=F32)

    def in_blocks(w):
        w = w.reshape(nck, SSM_CHUNK_GROUPS, SSM_STATE, SSM_GROUP)
        return jnp.einsum('kgpc,gh->kgchp', w, eye).reshape(nck, SSM_CHUNK_IN, SSM_CHUNK_STATE)

    def out_blocks(w):
        w = w.reshape(nck, SSM_CHUNK_GROUPS, SSM_GROUP, SSM_STATE)
        return jnp.einsum('kgcp,gh->kgphc', w, eye).reshape(nck, SSM_CHUNK_STATE, SSM_CHUNK_IN)

    wb = jnp.concatenate([in_blocks(bbar_r), in_blocks(bbar_i)], axis=2).astype(BF16)
    cb = jnp.concatenate([out_blocks(c_re.astype(F32)), -out_blocks(c_im.astype(F32))], axis=1).astype(BF16)
    sr, si = ar, ai
    for _ in range(int(math.log2(s_len))):
        sr, si = sr * sr - si * si, 2.0 * sr * si
    rows = [v.reshape(nck, 1, SSM_CHUNK_STATE) for v in (ar, ai, sr, si)]
    rows += [jnp.zeros_like(rows[0])] * (SUBLANES - len(rows))
    return wb, cb, jnp.concatenate(rows, axis=1)


def _s5_post_kernel(y_ref, u_ref, z_ref, x_ref, d_ref, wg_ref, bg_ref, wo_ref, lg_ref, lb_ref,
                    xo_ref, xb_ref, *, alpha):
    unblock = lambda ref: jnp.concatenate([ref[j] for j in range(ref.shape[0])], axis=1)
    y = unblock(y_ref) + d_ref[...] * unblock(u_ref)
    g = jax.nn.gelu(y)
    gl = jnp.dot(g.astype(BF16), wg_ref[...], preferred_element_type=F32) + bg_ref[...]
    y2 = g * _sigmoid(gl)
    z = unblock(z_ref)
    a = (y2 * (z * _sigmoid(z))).astype(BF16)
    o = jnp.dot(a, wo_ref[...], preferred_element_type=F32)
    xn = _layer_norm(alpha * x_ref[...] + o, lg_ref[...], lb_ref[...])
    xo_ref[...] = xn
    xb_ref[...] = xn.astype(BF16)


def _s5_post(yssm, uz, x, d, w_glu, b_glu, w_out, ln_g, ln_b, alpha, tm=256):
    nlb, m, _ = yssm.shape
    e = nlb * LANES
    dm = x.shape[1]
    tm = min(tm, m)
    row = lambda i: (i, 0)
    fixed = lambda i: (0, 0)
    return pl.pallas_call(
        functools.partial(_s5_post_kernel, alpha=alpha),
        grid=(m // tm,),
        in_specs=[pl.BlockSpec((nlb, tm, LANES), lambda i: (0, i, 0)),
                  pl.BlockSpec((nlb, tm, LANES), lambda i: (0, i, 0)),
                  pl.BlockSpec((nlb, tm, LANES), lambda i: (1, i, 0)),
                  pl.BlockSpec((tm, dm), row),
                  pl.BlockSpec((1, e), fixed),
                  pl.BlockSpec((e, e), fixed),
                  pl.BlockSpec((1, e), fixed),
                  pl.BlockSpec((e, dm), fixed),
                  pl.BlockSpec((1, dm), fixed),
                  pl.BlockSpec((1, dm), fixed)],
        out_specs=[pl.BlockSpec((tm, dm), row), pl.BlockSpec((tm, dm), row)],
        out_shape=[jax.ShapeDtypeStruct((m, dm), F32), jax.ShapeDtypeStruct((m, dm), BF16)],
        compiler_params=_params("parallel"),
        name="s5_post",
    )(yssm, uz, uz, x, d, w_glu, b_glu, w_out, ln_g, ln_b)


def _compress_kernel(t_ref, pos_ref, w1_ref, w2_ref, o_ref, q_ref, *, nc):
    half = CMP_BLOCK // 2
    acc_p = jnp.zeros((nc, HEAD_DIM), F32)
    acc_q = jnp.zeros((nc, HEAD_DIM), F32)
    for l in range(half):
        t = t_ref[pl.ds(l, nc, stride=CMP_STRIDE), :]
        lo = (t + pos_ref[l:l + 1, :]).astype(BF16)
        hi = (t + pos_ref[half + l:half + l + 1, :]).astype(BF16)
        acc_p += jnp.dot(lo, w1_ref[l * HEAD_DIM:(l + 1) * HEAD_DIM, :], preferred_element_type=F32)
        acc_q += jnp.dot(hi, w1_ref[(half + l) * HEAD_DIM:(half + l + 1) * HEAD_DIM, :],
                         preferred_element_type=F32)
    q_ref[0:nc, :] = acc_q
    q_ref[nc:nc + SUBLANES, :] = jnp.zeros((SUBLANES, HEAD_DIM), F32)
    pre = acc_p + q_ref[pl.ds(1, nc), :]
    mid = jax.nn.gelu(pre).astype(BF16)
    o_ref[...] = jnp.dot(mid, w2_ref[...], preferred_element_type=F32).astype(o_ref.dtype)


def _compress(kvc, pos, w1, w2, bsz, seq):
    nc = seq // CMP_STRIDE
    g = N_KV_HEADS
    return pl.pallas_call(
        functools.partial(_compress_kernel, nc=nc),
        grid=(bsz, 2, g),
        in_specs=[pl.BlockSpec((seq, HEAD_DIM), lambda b, s, h: (b, s * g + h)),
                  pl.BlockSpec((None, CMP_BLOCK, HEAD_DIM), lambda b, s, h: (s, 0, 0)),
                  pl.BlockSpec((None, CMP_BLOCK * HEAD_DIM, HEAD_DIM), lambda b, s, h: (s, 0, 0)),
                  pl.BlockSpec((None, HEAD_DIM, HEAD_DIM), lambda b, s, h: (s, 0, 0))],
        out_specs=pl.BlockSpec((None, None, None, nc, HEAD_DIM), lambda b, s, h: (b, s, h, 0, 0)),
        out_shape=jax.ShapeDtypeStruct((bsz, 2, g, nc, HEAD_DIM), BF16),
        scratch_shapes=[pltpu.VMEM((nc + SUBLANES, HEAD_DIM), F32)],
        compiler_params=_params("parallel", "parallel", "parallel"),
        name="kv_compress",
    )(kvc, pos, w1, w2)


def _cmp_select_kernel(q_ref, k_ref, v_ref, o_ref, sel_ref, *, tq, nc, ns, n_sel):
    q0 = pl.program_id(2) * tq
    k = k_ref[...]
    v = v_ref[...]
    t = q0 + lax.broadcasted_iota(jnp.int32, (tq, nc), 0)
    n = lax.broadcasted_iota(jnp.int32, (tq, nc), 1)
    valid = n * CMP_STRIDE + (CMP_BLOCK - 1) <= t
    psum = jnp.zeros((tq, nc), F32)
    for h in range(HEADS_PER_KV):
        qh = q_ref[:, h * HEAD_DIM:(h + 1) * HEAD_DIM]
        s = lax.dot_general(qh, k, (((1,), (1,)), ((), ())), preferred_element_type=F32)
        s = jnp.where(valid, s, MASK_VALUE)
        m = jnp.max(s, axis=-1, keepdims=True)
        e = jnp.where(valid, jnp.exp(s - m), 0.0)
        p = e / jnp.maximum(jnp.sum(e, axis=-1, keepdims=True), 1e-30)
        o_ref[:, h * HEAD_DIM:(h + 1) * HEAD_DIM] = jnp.dot(p.astype(BF16), v, preferred_element_type=F32)
        psum = psum + p

    nn = lax.broadcasted_iota(jnp.int32, (nc, LANES), 0)
    jj = lax.broadcasted_iota(jnp.int32, (nc, LANES), 1)
    overlap = ((nn * CMP_STRIDE < (jj + 1) * SEL_BLOCK)
               & (nn * CMP_STRIDE + (CMP_BLOCK - 1) >= jj * SEL_BLOCK)
               & (jj < ns)).astype(BF16)
    p_hi = psum.astype(BF16)
    p_lo = (psum - p_hi.astype(F32)).astype(BF16)
    imp = (jnp.dot(p_hi, overlap, preferred_element_type=F32)
           + jnp.dot(p_lo, overlap, preferred_element_type=F32))

    tt = q0 + lax.broadcasted_iota(jnp.int32, (tq, LANES), 0)
    j = lax.broadcasted_iota(jnp.int32, (tq, LANES), 1)
    cur = tt // SEL_BLOCK
    forced = (j == 0) | (j == cur) | (j == cur - 1)
    imp = jnp.where(j > cur, MASK_VALUE, jnp.where(forced, FORCE_SCORE, imp))
    imp = jnp.where(j < ns, imp, PAD_SCORE)

    x = imp.T
    nblk = ns // SUBLANES
    xs = [x[b * SUBLANES:(b + 1) * SUBLANES, :] for b in range(nblk)]
    cnts = [jnp.zeros((SUBLANES, tq), jnp.int32) for _ in range(nblk)]
    sub = lax.broadcasted_iota(jnp.int32, (SUBLANES, tq), 0)
    for i in range(ns):
        xi = x[i:i + 1, :]
        for b in range(nblk):
            if b * SUBLANES > i:
                beats = xi >= xs[b]
            elif (b + 1) * SUBLANES - 1 <= i:
                beats = xi > xs[b]
            else:
                beats = (xi > xs[b]) | ((xi >= xs[b]) & (sub > i - b * SUBLANES))
            cnts[b] = cnts[b] + beats.astype(jnp.int32)
    sel_t = jnp.concatenate([(c < n_sel).astype(F32) for c in cnts]
                            + [jnp.zeros((LANES - ns, tq), F32)] * (ns < LANES), axis=0)
    sel_ref[...] = sel_t.T.astype(sel_ref.dtype)


def _cmp_select(q, cmp_kv, bsz, seq):
    tq = min(CMP_TQ, seq)
    nq = seq // tq
    nc = seq // CMP_STRIDE
    ns = seq // SEL_BLOCK
    assert ns <= LANES and ns % SUBLANES == 0 and tq % LANES == 0
    g = N_KV_HEADS
    return pl.pallas_call(
        functools.partial(_cmp_select_kernel, tq=tq, nc=nc, ns=ns, n_sel=min(N_SELECT, ns)),
        grid=(bsz, g, nq),
        in_specs=[pl.BlockSpec((tq, GROUP_WIDTH), lambda b, h, i: (b * nq + i, h)),
                  pl.BlockSpec((None, None, None, nc, HEAD_DIM), lambda b, h, i: (b, 0, h, 0, 0)),
                  pl.BlockSpec((None, None, None, nc, HEAD_DIM), lambda b, h, i: (b, 1, h, 0, 0))],
        out_specs=[pl.BlockSpec((tq, GROUP_WIDTH), lambda b, h, i: (b * nq + i, h)),
                   pl.BlockSpec((None, None, tq, LANES), lambda b, h, i: (b, h, i, 0))],
        out_shape=[jax.ShapeDtypeStruct((bsz * seq, N_HEADS * HEAD_DIM), F32),
                   jax.ShapeDtypeStruct((bsz, g, seq, LANES), BF16)],
        compiler_params=_params("parallel", "parallel", "parallel"),
        name="nsa_cmp_select",
    )(q, cmp_kv, cmp_kv)


def _lane_chunks(v):
    return [v[:, c * LANES:(c + 1) * LANES] for c in range(v.shape[1] // LANES)]


def _softmax_pv(chunks, v):
    mx = chunks[0]
    for sc in chunks[1:]:
        mx = jnp.maximum(mx, sc)
    m = jnp.broadcast_to(jnp.max(mx, axis=-1, keepdims=True), mx.shape)
    es = [jnp.exp2(sc - m) for sc in chunks]
    ls = es[0]
    for e in es[1:]:
        ls = ls + e
    e = jnp.concatenate([e.astype(BF16) for e in es], axis=1)
    return jnp.dot(e, v, preferred_element_type=F32) / jnp.sum(ls, axis=-1, keepdims=True)


def _slc_win_kernel(q_ref, sel_ref, oh_ref, ks_ref, vs_ref, kw_ref, vw_ref, os_ref, ow_ref, s_ref, *,
                    tq, tk, seq):
    hp = HEADS_PER_KV
    rows = hp * tq
    q0 = pl.program_id(2) * tq
    q4 = jnp.concatenate([q_ref[:, h * HEAD_DIM:(h + 1) * HEAD_DIM] for h in range(hp)], axis=0)
    neg = ((sel_ref[...].astype(F32) - 1.0) * (-MASK_VALUE)).astype(BF16)
    q_aug = jnp.concatenate([q4, jnp.concatenate([neg] * hp, axis=0)], axis=1)
    nt_dims = (((1,), (1,)), ((), ()))
    n_tiles = (q0 + tq - 1) // tk + 1
    last = n_tiles - 1

    def masked_scores(kt):
        k0 = pl.multiple_of(kt * tk, tk)
        k_aug = jnp.concatenate([ks_ref[pl.ds(k0, tk), :], oh_ref[pl.ds(k0, tk), :]], axis=1)
        s = lax.dot_general(q_aug, k_aug, nt_dims, preferred_element_type=F32)
        return s * LOG2_E

    def lane_max(s, mx):
        for sc in _lane_chunks(s):
            mx = jnp.maximum(mx, sc)
        return mx

    def scores(kt, mx):
        s = masked_scores(kt)
        s_ref[kt] = s
        return lane_max(s, mx)

    mx = lax.fori_loop(0, last, scores, jnp.full((rows, LANES), MASK_VALUE, F32))
    t = q0 + lax.broadcasted_iota(jnp.int32, (tq, tk), 0)
    kcol = last * tk + lax.broadcasted_iota(jnp.int32, (tq, tk), 1)
    s = jnp.where((kcol <= t)[None], masked_scores(last).reshape(hp, tq, tk), MASK_VALUE).reshape(rows, tk)
    s_ref[last] = s
    mx = lane_max(s, mx)
    m = jnp.broadcast_to(jnp.max(mx, axis=-1, keepdims=True), (rows, LANES))

    def weighted(kt, carry):
        ls, acc = carry
        k0 = pl.multiple_of(kt * tk, tk)
        es = [jnp.exp2(sc - m) for sc in _lane_chunks(s_ref[kt])]
        for e in es:
            ls = ls + e
        e = jnp.concatenate([e.astype(BF16) for e in es], axis=1)
        return ls, acc + jnp.dot(e, vs_ref[pl.ds(k0, tk), :], preferred_element_type=F32)

    ls, acc = lax.fori_loop(0, n_tiles, weighted,
                            (jnp.zeros((rows, LANES), F32), jnp.zeros((rows, HEAD_DIM), F32)))
    o = acc / jnp.sum(ls, axis=-1, keepdims=True)
    for h in range(hp):
        os_ref[:, h * HEAD_DIM:(h + 1) * HEAD_DIM] = o[h * tq:(h + 1) * tq, :]

    tw = min(WIN_TQ, tq)
    span = min(WINDOW + tw, seq)
    for j in range(tq // tw):
        qw0 = q0 + j * tw
        start = pl.multiple_of(jnp.clip(qw0 + tw - span, 0, seq - span), tw)
        kpos = start + lax.broadcasted_iota(jnp.int32, (tw, span), 1)
        tpos = qw0 + lax.broadcasted_iota(jnp.int32, (tw, span), 0)
        wbias = jnp.where((kpos <= tpos) & (kpos > tpos - WINDOW), 0.0, MASK_VALUE)
        qw = jnp.concatenate([q_ref[j * tw:(j + 1) * tw, h * HEAD_DIM:(h + 1) * HEAD_DIM] for h in range(hp)],
                             axis=0)
        sw = lax.dot_general(qw, kw_ref[pl.ds(start, span), :], nt_dims, preferred_element_type=F32) * LOG2_E
        sw = (sw.reshape(hp, tw, span) + wbias[None]).reshape(hp * tw, span)
        ow = _softmax_pv(_lane_chunks(sw), vw_ref[pl.ds(start, span), :])
        for h in range(hp):
            ow_ref[j * tw:(j + 1) * tw, h * HEAD_DIM:(h + 1) * HEAD_DIM] = ow[h * tw:(h + 1) * tw, :]


def _slc_win(q, sel, kvs, bsz, seq):
    tq = min(ATT_TQ, seq)
    tk = min(ATT_TK, seq)
    nq = seq // tq
    g = N_KV_HEADS
    block_onehot = (jnp.arange(seq)[:, None] // SEL_BLOCK == jnp.arange(LANES)[None, :]).astype(BF16)
    kv_spec = lambda part: pl.BlockSpec((seq, HEAD_DIM), lambda b, h, i: (b, part * g + h))
    out_spec = pl.BlockSpec((tq, GROUP_WIDTH), lambda b, h, i: (b * nq + i, h))
    out_sds = jax.ShapeDtypeStruct((bsz * seq, N_HEADS * HEAD_DIM), F32)
    return pl.pallas_call(
        functools.partial(_slc_win_kernel, tq=tq, tk=tk, seq=seq),
        grid=(bsz, g, nq),
        in_specs=[pl.BlockSpec((tq, GROUP_WIDTH), lambda b, h, i: (b * nq + i, h)),
                  pl.BlockSpec((None, None, tq, LANES), lambda b, h, i: (b, h, i, 0)),
                  pl.BlockSpec((seq, LANES), lambda b, h, i: (0, 0)),
                  kv_spec(0), kv_spec(1), kv_spec(2), kv_spec(3)],
        out_specs=[out_spec, out_spec],
        out_shape=[out_sds, out_sds],
        scratch_shapes=[pltpu.VMEM((seq // tk, HEADS_PER_KV * tq, tk), F32)],
        compiler_params=pltpu.CompilerParams(dimension_semantics=("parallel", "parallel", "parallel"),
                                             vmem_limit_bytes=ATT_VMEM_LIMIT_BYTES),
        name="nsa_slc_win",
    )(q, sel, block_onehot, kvs, kvs, kvs, kvs)


def _nsa_post_kernel(oc_ref, os_ref, ow_ref, z_ref, gl_ref, x_ref, wo_ref, lg_ref, lb_ref,
                     xo_ref, xb_ref, *, alpha):
    width = N_HEADS * HEAD_DIM
    gate = _sigmoid(gl_ref[...])
    parts = []
    for h in range(N_HEADS):
        hs = slice(h * HEAD_DIM, (h + 1) * HEAD_DIM)
        acc = None
        for br, o_ref in enumerate((oc_ref, os_ref, ow_ref)):
            gcol = gate[:, br * N_HEADS + h:br * N_HEADS + h + 1]
            z = z_ref[:, br * width + h * HEAD_DIM:br * width + (h + 1) * HEAD_DIM]
            term = gcol * o_ref[:, hs] * (z * _sigmoid(z))
            acc = term if acc is None else acc + term
        parts.append(acc.astype(BF16))
    y = jnp.concatenate(parts, axis=1)
    o = jnp.dot(y, wo_ref[...], preferred_element_type=F32)
    xn = _layer_norm(alpha * x_ref[...] + o, lg_ref[...], lb_ref[...])
    xo_ref[...] = xn
    xb_ref[...] = xn.astype(BF16)


def _nsa_post(o_cmp, o_slc, o_win, z, gate_logits, x, w_out, ln_g, ln_b, alpha, tm=128):
    m, dm = x.shape
    width = N_HEADS * HEAD_DIM
    tm = min(tm, m)
    row = lambda i: (i, 0)
    fixed = lambda i: (0, 0)
    return pl.pallas_call(
        functools.partial(_nsa_post_kernel, alpha=alpha),
        grid=(m // tm,),
        in_specs=[pl.BlockSpec((tm, width), row),
                  pl.BlockSpec((tm, width), row),
                  pl.BlockSpec((tm, width), row),
                  pl.BlockSpec((tm, N_BRANCH * width), row),
                  pl.BlockSpec((tm, LANES), row),
                  pl.BlockSpec((tm, dm), row),
                  pl.BlockSpec((width, dm), fixed),
                  pl.BlockSpec((1, dm), fixed),
                  pl.BlockSpec((1, dm), fixed)],
        out_specs=[pl.BlockSpec((tm, dm), row), pl.BlockSpec((tm, dm), row)],
        out_shape=[jax.ShapeDtypeStruct((m, dm), F32), jax.ShapeDtypeStruct((m, dm), BF16)],
        compiler_params=_params("parallel"),
        name="nsa_post",
    )(o_cmp, o_slc, o_win, z, gate_logits, x, w_out, ln_g, ln_b)


def kernel(x, a_w_in, a_lam_re, a_lam_im, a_log_dt, a_b_re, a_b_im, a_c_re, a_c_im, a_d, a_w_glu, a_b_glu, a_w_out, kv_w, cmp_pos_k, cmp_w1_k, cmp_w2_k, cmp_pos_v, cmp_w1_v, cmp_w2_v, b_w_in, b_w_out, ln_g, ln_b):
    bsz, seq, dm = x.shape
    n_a = a_w_in.shape[0]
    n_b = b_w_in.shape[0]
    depth = n_a + n_b
    alpha = (2 * depth) ** 0.25
    width = N_HEADS * HEAD_DIM
    kvw = N_KV_HEADS * HEAD_DIM
    s_len = min(SCAN_T, seq) // SUBLANES

    xf = x.reshape(bsz * seq, dm).astype(F32)
    xb = xf.astype(BF16)

    for i in range(n_a):
        wb, cb, acoef = _s5_constants(a_lam_re[i], a_lam_im[i], a_log_dt[i], a_b_re[i], a_b_im[i],
                                      a_c_re[i], a_c_im[i], s_len)
        e = a_w_glu.shape[1]
        uz = _matmul(xb, a_w_in.astype(F32), F32, layer=i, lane_blocked=True)
        yssm = _s5_scan(uz, wb, cb, acoef, bsz, seq)
        xf, xb = _s5_post(yssm, uz, xf, a_d[i].reshape(1, e).astype(F32), a_w_glu[i].astype(BF16),
                          a_b_glu[i].reshape(1, e).astype(F32), a_w_out[i].astype(BF16),
                          ln_g[i].reshape(1, dm).astype(F32), ln_b[i].reshape(1, dm).astype(F32), alpha)

    kvc = _matmul(xb, kv_w.astype(F32), F32, n=2 * kvw)
    kvs = _matmul(xb, kv_w.astype(F32), BF16, col0=2 * kvw, n=4 * kvw)
    cmp_kv = _compress(kvc,
                       jnp.stack([cmp_pos_k, cmp_pos_v]).astype(F32),
                       jnp.stack([cmp_w1_k, cmp_w1_v]).astype(BF16),
                       jnp.stack([cmp_w2_k, cmp_w2_v]).astype(BF16), bsz, seq)

    for i in range(n_b):
        layer = n_a + i
        w_in = b_w_in[i]
        n_main = (1 + N_BRANCH) * width
        w_main = w_in[:, :n_main].astype(BF16)
        w_gate = jnp.pad(w_in[:, n_main:], ((0, 0), (0, LANES - (w_in.shape[1] - n_main)))).astype(BF16)
        q = _matmul(xb, w_main, BF16, n=width, scale=HEAD_DIM ** -0.5)
        z = _matmul(xb, w_main, F32, col0=width, n=N_BRANCH * width)
        gate_logits = _matmul(xb, w_gate, F32)
        o_cmp, sel = _cmp_select(q, cmp_kv, bsz, seq)
        o_slc, o_win = _slc_win(q, sel, kvs, bsz, seq)
        xf, xb = _nsa_post(o_cmp, o_slc, o_win, z, gate_logits, xf, b_w_out[i].astype(BF16),
                           ln_g[layer].reshape(1, dm).astype(F32), ln_b[layer].reshape(1, dm).astype(F32), alpha)

    return xf.reshape(bsz, seq, dm).astype(x.dtype)
```

```python
import functools
import math

import jax
import jax.numpy as jnp
from jax import lax
from jax.experimental import pallas as pl
from jax.experimental.pallas import tpu as pltpu

F32 = jnp.float32
BF16 = jnp.bfloat16

HEAD_DIM = 128
N_HEADS = 16
N_KV_HEADS = 4
HEADS_PER_KV = N_HEADS // N_KV_HEADS
GROUP_WIDTH = HEADS_PER_KV * HEAD_DIM
N_BRANCH = 3
CMP_BLOCK = 32
CMP_STRIDE = 16
SEL_BLOCK = 64
N_SELECT = 16
WINDOW = 512
SSM_GROUP = 16
SSM_STATE = 64
SSM_CHUNK_GROUPS = 16
SSM_CHUNK_IN = SSM_CHUNK_GROUPS * SSM_GROUP
SSM_CHUNK_STATE = SSM_CHUNK_GROUPS * SSM_STATE
LN_EPS = 1e-5
MASK_VALUE = -1e30
FORCE_SCORE = 1e6
PAD_SCORE = -3e38
LOG2_E = math.log2(math.e)

SUBLANES = 8
LANES = 128
VMEM_LIMIT_BYTES = 48 * 1024 * 1024

SCAN_T = 256
CMP_TQ = 512
ATT_TQ = 512
ATT_TK = 512
WIN_TQ = 128
ATT_VMEM_LIMIT_BYTES = 58 * 1024 * 1024


def _params(*sem):
    return pltpu.CompilerParams(dimension_semantics=sem, vmem_limit_bytes=VMEM_LIMIT_BYTES)


def _sigmoid(v):
    return 1.0 / (1.0 + jnp.exp(-v))


def _layer_norm(v, g, b):
    mu = jnp.mean(v, axis=-1, keepdims=True)
    d = v - mu
    var = jnp.mean(d * d, axis=-1, keepdims=True)
    return d * lax.rsqrt(var + LN_EPS) * g + b


def _mm_kernel(a_ref, w_ref, o_ref, *wb_refs, scale, lane_blocked):
    if wb_refs:
        wb_ref, = wb_refs

        @pl.when(pl.program_id(1) == 0)
        def _():
            wb_ref[...] = w_ref[...].astype(BF16)

        w_ref = wb_ref
    acc = jnp.dot(a_ref[...], w_ref[...], preferred_element_type=F32)
    if scale is not None:
        acc = acc * scale
    if lane_blocked:
        for j in range(o_ref.shape[0]):
            o_ref[j] = acc[:, j * LANES:(j + 1) * LANES].astype(o_ref.dtype)
    else:
        o_ref[...] = acc.astype(o_ref.dtype)


def _matmul(a, w, out_dtype, *, n=None, col0=0, layer=None, tm=512, tn=1024, scale=None, lane_blocked=False):
    m, k = a.shape
    n = w.shape[-1] if n is None else n
    tm = min(tm, m)
    tn = min(tn, n)
    assert m % tm == 0 and n % tn == 0 and col0 % tn == 0
    jb = col0 // tn
    if layer is None:
        w_spec = pl.BlockSpec((k, tn), lambda j, i: (0, j + jb))
    else:
        w_spec = pl.BlockSpec((None, k, tn), lambda j, i: (layer, 0, j + jb))
    if lane_blocked:
        out_spec = pl.BlockSpec((tn // LANES, tm, LANES), lambda j, i: (j, i, 0))
        out_shape = jax.ShapeDtypeStruct((n // LANES, m, LANES), out_dtype)
    else:
        out_spec = pl.BlockSpec((tm, tn), lambda j, i: (i, j))
        out_shape = jax.ShapeDtypeStruct((m, n), out_dtype)
    return pl.pallas_call(
        functools.partial(_mm_kernel, scale=scale, lane_blocked=lane_blocked),
        grid=(n // tn, m // tm),
        in_specs=[pl.BlockSpec((tm, k), lambda j, i: (i, 0)), w_spec],
        out_specs=out_spec,
        out_shape=out_shape,
        scratch_shapes=[] if w.dtype == BF16 else [pltpu.VMEM((k, tn), BF16)],
        compiler_params=_params("parallel", "arbitrary"),
        name="matmul",
    )(a, w)


def _s5_scan_kernel(u_ref, wb_ref, cb_ref, a_ref, o_ref, bu_ref, hb_ref, carry_ref, *, t_tok):
    s_len = t_tok // SUBLANES
    p = SSM_CHUNK_STATE

    @pl.when(pl.program_id(2) == 0)
    def _():
        carry_ref[...] = jnp.zeros_like(carry_ref)
        hb_ref[...] = jnp.zeros_like(hb_ref)

    n_lb = SSM_CHUNK_IN // LANES
    rows = [jnp.concatenate([u_ref[c, pl.ds(k, SUBLANES, stride=s_len), :] for c in range(n_lb)], axis=1)
            for k in range(s_len)]
    up = jnp.concatenate(rows, axis=0).astype(BF16)
    bu_ref[...] = jnp.dot(up, wb_ref[...], preferred_element_type=F32)

    y = jnp.dot(hb_ref[...].astype(BF16), cb_ref[...], preferred_element_type=F32)
    for k in range(s_len):
        for c in range(n_lb):
            o_ref[c, pl.ds(k, SUBLANES, stride=s_len), :] = y[k * SUBLANES:(k + 1) * SUBLANES,
                                                               c * LANES:(c + 1) * LANES]

    sub = lax.broadcasted_iota(jnp.int32, (SUBLANES, LANES), 0)
    for blk in range(p // LANES):
        re = slice(2 * blk * LANES, (2 * blk + 1) * LANES)
        im = slice((2 * blk + 1) * LANES, (2 * blk + 2) * LANES)
        st = slice(blk * LANES, (blk + 1) * LANES)
        ar = jnp.broadcast_to(a_ref[0:1, st], (SUBLANES, LANES))
        ai = jnp.broadcast_to(a_ref[1:2, st], (SUBLANES, LANES))
        sr = jnp.broadcast_to(a_ref[2:3, st], (SUBLANES, LANES))
        si = jnp.broadcast_to(a_ref[3:4, st], (SUBLANES, LANES))

        def step(k, hr, hi):
            br = bu_ref[k * SUBLANES:(k + 1) * SUBLANES, re]
            bi = bu_ref[k * SUBLANES:(k + 1) * SUBLANES, im]
            return ar * hr - ai * hi + br, ar * hi + ai * hr + bi

        er = jnp.zeros((SUBLANES, LANES), F32)
        ei = jnp.zeros((SUBLANES, LANES), F32)
        for k in range(s_len):
            er, ei = step(k, er, ei)

        hinr = jnp.where(sub == 0, carry_ref[:, re], 0.0)
        hini = jnp.where(sub == 0, carry_ref[:, im], 0.0)
        for j in range(SUBLANES - 1):
            nr = sr * hinr - si * hini + er
            ni = sr * hini + si * hinr + ei
            hinr = jnp.where(sub == j + 1, pltpu.roll(nr, 1, 0), hinr)
            hini = jnp.where(sub == j + 1, pltpu.roll(ni, 1, 0), hini)
        nr = sr * hinr - si * hini + er
        ni = sr * hini + si * hinr + ei
        carry_ref[:, re] = jnp.broadcast_to(nr[SUBLANES - 1:SUBLANES, :], (SUBLANES, LANES))
        carry_ref[:, im] = jnp.broadcast_to(ni[SUBLANES - 1:SUBLANES, :], (SUBLANES, LANES))

        hr, hi = hinr, hini
        for k in range(s_len):
            hr, hi = step(k, hr, hi)
            hb_ref[k * SUBLANES:(k + 1) * SUBLANES, re] = hr
            hb_ref[k * SUBLANES:(k + 1) * SUBLANES, im] = hi


def _s5_scan(uz, wb, cb, acoef, bsz, seq):
    n_chunks = wb.shape[0]
    e = n_chunks * SSM_CHUNK_IN
    n_lb = SSM_CHUNK_IN // LANES
    t_tok = min(SCAN_T, seq)
    nt = seq // t_tok
    assert seq % t_tok == 0 and t_tok % (SUBLANES * SUBLANES) == 0
    return pl.pallas_call(
        functools.partial(_s5_scan_kernel, t_tok=t_tok),
        grid=(bsz, n_chunks, nt + 1),
        in_specs=[
            pl.BlockSpec((n_lb, t_tok, LANES), lambda b, c, t: (c, b * nt + jnp.minimum(t, nt - 1), 0)),
            pl.BlockSpec((None, SSM_CHUNK_IN, 2 * SSM_CHUNK_STATE), lambda b, c, t: (c, 0, 0)),
            pl.BlockSpec((None, 2 * SSM_CHUNK_STATE, SSM_CHUNK_IN), lambda b, c, t: (c, 0, 0)),
            pl.BlockSpec((None, SUBLANES, SSM_CHUNK_STATE), lambda b, c, t: (c, 0, 0)),
        ],
        out_specs=pl.BlockSpec((n_lb, t_tok, LANES), lambda b, c, t: (c, b * nt + jnp.maximum(t - 1, 0), 0)),
        out_shape=jax.ShapeDtypeStruct((e // LANES, bsz * seq, LANES), F32),
        scratch_shapes=[pltpu.VMEM((t_tok, 2 * SSM_CHUNK_STATE), F32),
                        pltpu.VMEM((t_tok, 2 * SSM_CHUNK_STATE), F32),
                        pltpu.VMEM((SUBLANES, 2 * SSM_CHUNK_STATE), F32)],
        compiler_params=_params("parallel", "parallel", "arbitrary"),
        name="s5_scan",
    )(uz, wb, cb, acoef)


def _s5_constants(lam_re, lam_im, log_dt, b_re, b_im, c_re, c_im, s_len):
    lr = lam_re.astype(F32)
    li = lam_im.astype(F32)
    dt = jnp.exp(log_dt.astype(F32))[:, None]
    mag = jnp.exp(lr * dt)
    ar = mag * jnp.cos(li * dt)
    ai = mag * jnp.sin(li * dt)
    inv_abs2 = 1.0 / (lr * lr + li * li)
    cr = ((ar - 1.0) * lr + ai * li) * inv_abs2
    ci = (ai * lr - (ar - 1.0) * li) * inv_abs2
    br = b_re.astype(F32)
    bi = b_im.astype(F32)
    bbar_r = cr[..., None] * br - ci[..., None] * bi
    bbar_i = cr[..., None] * bi + ci[..., None] * br
    g = lr.shape[0]
    nck = g // SSM_CHUNK_GROUPS
    eye = jnp.eye(SSM_CHUNK_GROUPS, dtype=F32)

    def in_blocks(w):
        w = w.reshape(nck, SSM_CHUNK_GROUPS, SSM_STATE, SSM_GROUP)
        return jnp.einsum('kgpc,gh->kgchp', w, eye).reshape(nck, SSM_CHUNK_IN, SSM_CHUNK_STATE)

    def out_blocks(w):
        w = w.reshape(nck, SSM_CHUNK_GROUPS, SSM_GROUP, SSM_STATE)
        return jnp.einsum('kgcp,gh->kgphc', w, eye).reshape(nck, SSM_CHUNK_STATE, SSM_CHUNK_IN)

    nsb = SSM_CHUNK_STATE // LANES
    wb = jnp.stack([in_blocks(bbar_r).reshape(nck, SSM_CHUNK_IN, nsb, LANES),
                    in_blocks(bbar_i).reshape(nck, SSM_CHUNK_IN, nsb, LANES)], axis=3)
    wb = wb.reshape(nck, SSM_CHUNK_IN, 2 * SSM_CHUNK_STATE).astype(BF16)
    cb = jnp.stack([out_blocks(c_re.astype(F32)).reshape(nck, nsb, LANES, SSM_CHUNK_IN),
                    -out_blocks(c_im.astype(F32)).reshape(nck, nsb, LANES, SSM_CHUNK_IN)], axis=2)
    cb = cb.reshape(nck, 2 * SSM_CHUNK_STATE, SSM_CHUNK_IN).astype(BF16)
    sr, si = ar, ai
    for _ in range(int(math.log2(s_len))):
        sr, si = sr * sr - si * si, 2.0 * sr * si
    rows = [v.reshape(nck, 1, SSM_CHUNK_STATE) for v in (ar, ai, sr, si)]
    rows += [jnp.zeros_like(rows[0])] * (SUBLANES - len(rows))
    return wb, cb, jnp.concatenate(rows, axis=1)


def _s5_post_kernel(y_ref, u_ref, z_ref, x_ref, d_ref, wg_ref, bg_ref, wo_ref, lg_ref, lb_ref,
                    xo_ref, xb_ref, *, alpha):
    unblock = lambda ref: jnp.concatenate([ref[j] for j in range(ref.shape[0])], axis=1)
    y = unblock(y_ref) + d_ref[...] * unblock(u_ref)
    g = jax.nn.gelu(y)
    gl = jnp.dot(g.astype(BF16), wg_ref[...], preferred_element_type=F32) + bg_ref[...]
    y2 = g * _sigmoid(gl)
    z = unblock(z_ref)
    a = (y2 * (z * _sigmoid(z))).astype(BF16)
    o = jnp.dot(a, wo_ref[...], preferred_element_type=F32)
    xn = _layer_norm(alpha * x_ref[...] + o, lg_ref[...], lb_ref[...])
    xo_ref[...] = xn
    xb_ref[...] = xn.astype(BF16)


def _s5_post(yssm, uz, x, d, w_glu, b_glu, w_out, ln_g, ln_b, alpha, tm=256):
    nlb, m, _ = yssm.shape
    e = nlb * LANES
    dm = x.shape[1]
    tm = min(tm, m)
    row = lambda i: (i, 0)
    fixed = lambda i: (0, 0)
    return pl.pallas_call(
        functools.partial(_s5_post_kernel, alpha=alpha),
        grid=(m // tm,),
        in_specs=[pl.BlockSpec((nlb, tm, LANES), lambda i: (0, i, 0)),
                  pl.BlockSpec((nlb, tm, LANES), lambda i: (0, i, 0)),
                  pl.BlockSpec((nlb, tm, LANES), lambda i: (1, i, 0)),
                  pl.BlockSpec((tm, dm), row),
                  pl.BlockSpec((1, e), fixed),
                  pl.BlockSpec((e, e), fixed),
                  pl.BlockSpec((1, e), fixed),
                  pl.BlockSpec((e, dm), fixed),
                  pl.BlockSpec((1, dm), fixed),
                  pl.BlockSpec((1, dm), fixed)],
        out_specs=[pl.BlockSpec((tm, dm), row), pl.BlockSpec((tm, dm), row)],
        out_shape=[jax.ShapeDtypeStruct((m, dm), F32), jax.ShapeDtypeStruct((m, dm), BF16)],
        compiler_params=_params("parallel"),
        name="s5_post",
    )(yssm, uz, uz, x, d, w_glu, b_glu, w_out, ln_g, ln_b)


def _compress_kernel(t_ref, pos_ref, w1_ref, w2_ref, o_ref, q_ref, *, nc):
    half = CMP_BLOCK // 2
    acc_p = jnp.zeros((nc, HEAD_DIM), F32)
    acc_q = jnp.zeros((nc, HEAD_DIM), F32)
    for l in range(half):
        t = t_ref[pl.ds(l, nc, stride=CMP_STRIDE), :]
        lo = (t + pos_ref[l:l + 1, :]).astype(BF16)
        hi = (t + pos_ref[half + l:half + l + 1, :]).astype(BF16)
        acc_p += jnp.dot(lo, w1_ref[l * HEAD_DIM:(l + 1) * HEAD_DIM, :], preferred_element_type=F32)
        acc_q += jnp.dot(hi, w1_ref[(half + l) * HEAD_DIM:(half + l + 1) * HEAD_DIM, :],
                         preferred_element_type=F32)
    q_ref[0:nc, :] = acc_q
    q_ref[nc:nc + SUBLANES, :] = jnp.zeros((SUBLANES, HEAD_DIM), F32)
    pre = acc_p + q_ref[pl.ds(1, nc), :]
    mid = jax.nn.gelu(pre).astype(BF16)
    o_ref[...] = jnp.dot(mid, w2_ref[...], preferred_element_type=F32).astype(o_ref.dtype)


def _compress(kvc, pos, w1, w2, bsz, seq):
    nc = seq // CMP_STRIDE
    g = N_KV_HEADS
    return pl.pallas_call(
        functools.partial(_compress_kernel, nc=nc),
        grid=(bsz, 2, g),
        in_specs=[pl.BlockSpec((seq, HEAD_DIM), lambda b, s, h: (b, s * g + h)),
                  pl.BlockSpec((None, CMP_BLOCK, HEAD_DIM), lambda b, s, h: (s, 0, 0)),
                  pl.BlockSpec((None, CMP_BLOCK * HEAD_DIM, HEAD_DIM), lambda b, s, h: (s, 0, 0)),
                  pl.BlockSpec((None, HEAD_DIM, HEAD_DIM), lambda b, s, h: (s, 0, 0))],
        out_specs=pl.BlockSpec((None, None, None, nc, HEAD_DIM), lambda b, s, h: (b, s, h, 0, 0)),
        out_shape=jax.ShapeDtypeStruct((bsz, 2, g, nc, HEAD_DIM), BF16),
        scratch_shapes=[pltpu.VMEM((nc + SUBLANES, HEAD_DIM), F32)],
        compiler_params=_params("parallel", "parallel", "parallel"),
        name="kv_compress",
    )(kvc, pos, w1, w2)


def _cmp_select_kernel(q_ref, k_ref, v_ref, o_ref, sel_ref, *, tq, nc, ns, n_sel):
    hp = HEADS_PER_KV
    step = pl.program_id(2)
    q0 = step * tq
    q4 = jnp.concatenate([q_ref[:, h * HEAD_DIM:(h + 1) * HEAD_DIM] for h in range(hp)], axis=0)
    t = q0 + lax.broadcasted_iota(jnp.int32, (tq, nc), 0)
    n = lax.broadcasted_iota(jnp.int32, (tq, nc), 1)
    valid = (n * CMP_STRIDE + (CMP_BLOCK - 1) <= t)[None]
    s = lax.dot_general(q4, k_ref[...], (((1,), (1,)), ((), ())), preferred_element_type=F32)
    s = jnp.where(valid, s.reshape(hp, tq, nc), MASK_VALUE)
    m = jnp.max(s, axis=-1, keepdims=True)
    e = jnp.where(valid, jnp.exp(s - m), 0.0)
    p = e * (1.0 / jnp.maximum(jnp.sum(e, axis=-1, keepdims=True), 1e-30))
    o = jnp.dot(p.reshape(hp * tq, nc).astype(BF16), v_ref[...], preferred_element_type=F32)
    for h in range(hp):
        o_ref[:, h * HEAD_DIM:(h + 1) * HEAD_DIM] = o[h * tq:(h + 1) * tq, :]
    psum = p[0]
    for h in range(1, hp):
        psum = psum + p[h]

    nn = lax.broadcasted_iota(jnp.int32, (nc, LANES), 0)
    jj = lax.broadcasted_iota(jnp.int32, (nc, LANES), 1)
    overlap = ((nn * CMP_STRIDE < (jj + 1) * SEL_BLOCK)
               & (nn * CMP_STRIDE + (CMP_BLOCK - 1) >= jj * SEL_BLOCK)
               & (jj < ns)).astype(BF16)
    p_hi = psum.astype(BF16)
    p_lo = (psum - p_hi.astype(F32)).astype(BF16)
    imp = (jnp.dot(p_hi, overlap, preferred_element_type=F32)
           + jnp.dot(p_lo, overlap, preferred_element_type=F32))

    tt = q0 + lax.broadcasted_iota(jnp.int32, (tq, LANES), 0)
    j = lax.broadcasted_iota(jnp.int32, (tq, LANES), 1)
    cur = tt // SEL_BLOCK
    forced = (j == 0) | (j == cur) | (j == cur - 1)
    imp = jnp.where(j > cur, MASK_VALUE, jnp.where(forced, FORCE_SCORE, imp))
    imp = jnp.where(j < ns, imp, PAD_SCORE)

    x = imp.T
    sub = lax.broadcasted_iota(jnp.int32, (SUBLANES, tq), 0)

    def rank_and_store(live):
        nblk = live // SUBLANES
        if live <= n_sel:
            picked = [jnp.ones((SUBLANES, tq), F32)] * nblk
        else:
            xs = [x[b * SUBLANES:(b + 1) * SUBLANES, :] for b in range(nblk)]
            cnts = [jnp.zeros((SUBLANES, tq), jnp.int32) for _ in range(nblk)]
            for i in range(live):
                xi = x[i:i + 1, :]
                for b in range(nblk):
                    if b * SUBLANES > i:
                        beats = xi >= xs[b]
                    elif (b + 1) * SUBLANES - 1 <= i:
                        beats = xi > xs[b]
                    else:
                        beats = (xi > xs[b]) | ((xi >= xs[b]) & (sub > i - b * SUBLANES))
                    cnts[b] = cnts[b] + beats.astype(jnp.int32)
            picked = [(c < n_sel).astype(F32) for c in cnts]
        sel_t = jnp.concatenate(picked + [jnp.zeros((LANES - live, tq), F32)] * (live < LANES), axis=0)
        sel_ref[...] = sel_t.T.astype(sel_ref.dtype)

    for c in range(ns * SEL_BLOCK // tq):
        pl.when(step == c)(functools.partial(rank_and_store, (c + 1) * tq // SEL_BLOCK))


def _cmp_select(q, cmp_kv, bsz, seq):
    tq = min(CMP_TQ, seq)
    nq = seq // tq
    nc = seq // CMP_STRIDE
    ns = seq // SEL_BLOCK
    assert ns <= LANES and tq % LANES == 0 and tq % (SUBLANES * SEL_BLOCK) == 0
    g = N_KV_HEADS
    return pl.pallas_call(
        functools.partial(_cmp_select_kernel, tq=tq, nc=nc, ns=ns, n_sel=min(N_SELECT, ns)),
        grid=(bsz, g, nq),
        in_specs=[pl.BlockSpec((tq, GROUP_WIDTH), lambda b, h, i: (b * nq + i, h)),
                  pl.BlockSpec((None, None, None, nc, HEAD_DIM), lambda b, h, i: (b, 0, h, 0, 0)),
                  pl.BlockSpec((None, None, None, nc, HEAD_DIM), lambda b, h, i: (b, 1, h, 0, 0))],
        out_specs=[pl.BlockSpec((tq, GROUP_WIDTH), lambda b, h, i: (b * nq + i, h)),
                   pl.BlockSpec((None, None, tq, LANES), lambda b, h, i: (b, h, i, 0))],
        out_shape=[jax.ShapeDtypeStruct((bsz * seq, N_HEADS * HEAD_DIM), F32),
                   jax.ShapeDtypeStruct((bsz, g, seq, LANES), BF16)],
        compiler_params=_params("parallel", "parallel", "parallel"),
        name="nsa_cmp_select",
    )(q, cmp_kv, cmp_kv)


def _lane_chunks(v):
    return [v[:, c * LANES:(c + 1) * LANES] for c in range(v.shape[1] // LANES)]


def _softmax_pv(chunks, v):
    mx = chunks[0]
    for sc in chunks[1:]:
        mx = jnp.maximum(mx, sc)
    m = jnp.broadcast_to(jnp.max(mx, axis=-1, keepdims=True), mx.shape)
    es = [jnp.exp2(sc - m) for sc in chunks]
    ls = es[0]
    for e in es[1:]:
        ls = ls + e
    e = jnp.concatenate([e.astype(BF16) for e in es], axis=1)
    return jnp.dot(e, v, preferred_element_type=F32) / jnp.sum(ls, axis=-1, keepdims=True)


def _slc_win_kernel(q_ref, sel_ref, oh_ref, ks_ref, vs_ref, kw_ref, vw_ref, os_ref, ow_ref, s_ref, *,
                    tq, tk, seq):
    hp = HEADS_PER_KV
    rows = hp * tq
    q0 = pl.program_id(2) * tq
    q4 = jnp.concatenate([q_ref[:, h * HEAD_DIM:(h + 1) * HEAD_DIM] for h in range(hp)], axis=0)
    neg = ((sel_ref[...].astype(F32) - 1.0) * (-MASK_VALUE)).astype(BF16)
    q_aug = jnp.concatenate([q4, jnp.concatenate([neg] * hp, axis=0)], axis=1)
    nt_dims = (((1,), (1,)), ((), ()))
    n_tiles = (q0 + tq - 1) // tk + 1
    last = n_tiles - 1

    def masked_scores(kt):
        k0 = pl.multiple_of(kt * tk, tk)
        k_aug = jnp.concatenate([ks_ref[pl.ds(k0, tk), :], oh_ref[pl.ds(k0, tk), :]], axis=1)
        s = lax.dot_general(q_aug, k_aug, nt_dims, preferred_element_type=F32)
        return s * LOG2_E

    def lane_max(s, mx):
        for sc in _lane_chunks(s):
            mx = jnp.maximum(mx, sc)
        return mx

    def scores(kt, mx):
        s = masked_scores(kt)
        s_ref[kt] = s
        return lane_max(s, mx)

    mx = lax.fori_loop(0, last, scores, jnp.full((rows, LANES), MASK_VALUE, F32))
    t = q0 + lax.broadcasted_iota(jnp.int32, (tq, tk), 0)
    kcol = last * tk + lax.broadcasted_iota(jnp.int32, (tq, tk), 1)
    s = jnp.where((kcol <= t)[None], masked_scores(last).reshape(hp, tq, tk), MASK_VALUE).reshape(rows, tk)
    s_ref[last] = s
    mx = lane_max(s, mx)
    m = jnp.broadcast_to(jnp.max(mx, axis=-1, keepdims=True), (rows, LANES))

    def weighted(kt, carry):
        ls, acc = carry
        k0 = pl.multiple_of(kt * tk, tk)
        es = [jnp.exp2(sc - m) for sc in _lane_chunks(s_ref[kt])]
        for e in es:
            ls = ls + e
        e = jnp.concatenate([e.astype(BF16) for e in es], axis=1)
        return ls, acc + jnp.dot(e, vs_ref[pl.ds(k0, tk), :], preferred_element_type=F32)

    ls, acc = lax.fori_loop(0, n_tiles, weighted,
                            (jnp.zeros((rows, LANES), F32), jnp.zeros((rows, HEAD_DIM), F32)))
    o = acc / jnp.sum(ls, axis=-1, keepdims=True)
    for h in range(hp):
        os_ref[:, h * HEAD_DIM:(h + 1) * HEAD_DIM] = o[h * tq:(h + 1) * tq, :]

    tw = min(WIN_TQ, tq)
    span = min(WINDOW + tw, seq)
    for j in range(tq // tw):
        qw0 = q0 + j * tw
        start = pl.multiple_of(jnp.clip(qw0 + tw - span, 0, seq - span), tw)
        kpos = start + lax.broadcasted_iota(jnp.int32, (tw, span), 1)
        tpos = qw0 + lax.broadcasted_iota(jnp.int32, (tw, span), 0)
        wbias = jnp.where((kpos <= tpos) & (kpos > tpos - WINDOW), 0.0, MASK_VALUE)
        qw = jnp.concatenate([q_ref[j * tw:(j + 1) * tw, h * HEAD_DIM:(h + 1) * HEAD_DIM] for h in range(hp)],
                             axis=0)
        sw = lax.dot_general(qw, kw_ref[pl.ds(start, span), :], nt_dims, preferred_element_type=F32) * LOG2_E
        sw = (sw.reshape(hp, tw, span) + wbias[None]).reshape(hp * tw, span)
        ow = _softmax_pv(_lane_chunks(sw), vw_ref[pl.ds(start, span), :])
        for h in range(hp):
            ow_ref[j * tw:(j + 1) * tw, h * HEAD_DIM:(h + 1) * HEAD_DIM] = ow[h * tw:(h + 1) * tw, :]


def _slc_win(q, sel, kvs, bsz, seq):
    tq = min(ATT_TQ, seq)
    tk = min(ATT_TK, seq)
    nq = seq // tq
    g = N_KV_HEADS
    block_onehot = (jnp.arange(seq)[:, None] // SEL_BLOCK == jnp.arange(LANES)[None, :]).astype(BF16)
    kv_spec = lambda part: pl.BlockSpec((seq, HEAD_DIM), lambda b, h, i: (b, part * g + h))
    out_spec = pl.BlockSpec((tq, GROUP_WIDTH), lambda b, h, i: (b * nq + i, h))
    out_sds = jax.ShapeDtypeStruct((bsz * seq, N_HEADS * HEAD_DIM), F32)
    return pl.pallas_call(
        functools.partial(_slc_win_kernel, tq=tq, tk=tk, seq=seq),
        grid=(bsz, g, nq),
        in_specs=[pl.BlockSpec((tq, GROUP_WIDTH), lambda b, h, i: (b * nq + i, h)),
                  pl.BlockSpec((None, None, tq, LANES), lambda b, h, i: (b, h, i, 0)),
                  pl.BlockSpec((seq, LANES), lambda b, h, i: (0, 0)),
                  kv_spec(0), kv_spec(1), kv_spec(2), kv_spec(3)],
        out_specs=[out_spec, out_spec],
        out_shape=[out_sds, out_sds],
        scratch_shapes=[pltpu.VMEM((seq // tk, HEADS_PER_KV * tq, tk), F32)],
        compiler_params=pltpu.CompilerParams(dimension_semantics=("parallel", "parallel", "parallel"),
                                             vmem_limit_bytes=ATT_VMEM_LIMIT_BYTES),
        name="nsa_slc_win",
    )(q, sel, block_onehot, kvs, kvs, kvs, kvs)


def _nsa_post_kernel(oc_ref, os_ref, ow_ref, z_ref, gl_ref, x_ref, wo_ref, lg_ref, lb_ref,
                     xo_ref, xb_ref, *, alpha):
    width = N_HEADS * HEAD_DIM
    gate = _sigmoid(gl_ref[...])
    parts = []
    for h in range(N_HEADS):
        hs = slice(h * HEAD_DIM, (h + 1) * HEAD_DIM)
        acc = None
        for br, o_ref in enumerate((oc_ref, os_ref, ow_ref)):
            gcol = gate[:, br * N_HEADS + h:br * N_HEADS + h + 1]
            z = z_ref[:, br * width + h * HEAD_DIM:br * width + (h + 1) * HEAD_DIM]
            term = gcol * o_ref[:, hs] * (z * _sigmoid(z))
            acc = term if acc is None else acc + term
        parts.append(acc.astype(BF16))
    y = jnp.concatenate(parts, axis=1)
    o = jnp.dot(y, wo_ref[...], preferred_element_type=F32)
    xn = _layer_norm(alpha * x_ref[...] + o, lg_ref[...], lb_ref[...])
    xo_ref[...] = xn
    xb_ref[...] = xn.astype(BF16)


def _nsa_post(o_cmp, o_slc, o_win, z, gate_logits, x, w_out, ln_g, ln_b, alpha, tm=256):
    m, dm = x.shape
    width = N_HEADS * HEAD_DIM
    tm = min(tm, m)
    row = lambda i: (i, 0)
    fixed = lambda i: (0, 0)
    return pl.pallas_call(
        functools.partial(_nsa_post_kernel, alpha=alpha),
        grid=(m // tm,),
        in_specs=[pl.BlockSpec((tm, width), row),
                  pl.BlockSpec((tm, width), row),
                  pl.BlockSpec((tm, width), row),
                  pl.BlockSpec((tm, N_BRANCH * width), row),
                  pl.BlockSpec((tm, LANES), row),
                  pl.BlockSpec((tm, dm), row),
                  pl.BlockSpec((width, dm), fixed, pipeline_mode=pl.Buffered(1)),
                  pl.BlockSpec((1, dm), fixed),
                  pl.BlockSpec((1, dm), fixed)],
        out_specs=[pl.BlockSpec((tm, dm), row), pl.BlockSpec((tm, dm), row)],
        out_shape=[jax.ShapeDtypeStruct((m, dm), F32), jax.ShapeDtypeStruct((m, dm), BF16)],
        compiler_params=_params("parallel"),
        name="nsa_post",
    )(o_cmp, o_slc, o_win, z, gate_logits, x, w_out, ln_g, ln_b)


def kernel(x, a_w_in, a_lam_re, a_lam_im, a_log_dt, a_b_re, a_b_im, a_c_re, a_c_im, a_d, a_w_glu, a_b_glu, a_w_out, kv_w, cmp_pos_k, cmp_w1_k, cmp_w2_k, cmp_pos_v, cmp_w1_v, cmp_w2_v, b_w_in, b_w_out, ln_g, ln_b):
    bsz, seq, dm = x.shape
    n_a = a_w_in.shape[0]
    n_b = b_w_in.shape[0]
    depth = n_a + n_b
    alpha = (2 * depth) ** 0.25
    width = N_HEADS * HEAD_DIM
    kvw = N_KV_HEADS * HEAD_DIM
    s_len = min(SCAN_T, seq) // SUBLANES

    xf = x.reshape(bsz * seq, dm).astype(F32)
    xb = xf.astype(BF16)

    for i in range(n_a):
        wb, cb, acoef = _s5_constants(a_lam_re[i], a_lam_im[i], a_log_dt[i], a_b_re[i], a_b_im[i],
                                      a_c_re[i], a_c_im[i], s_len)
        e = a_w_glu.shape[1]
        uz = _matmul(xb, a_w_in.astype(F32), F32, layer=i, lane_blocked=True)
        yssm = _s5_scan(uz, wb, cb, acoef, bsz, seq)
        xf, xb = _s5_post(yssm, uz, xf, a_d[i].reshape(1, e).astype(F32), a_w_glu[i].astype(BF16),
                          a_b_glu[i].reshape(1, e).astype(F32), a_w_out[i].astype(BF16),
                          ln_g[i].reshape(1, dm).astype(F32), ln_b[i].reshape(1, dm).astype(F32), alpha)

    kvc = _matmul(xb, kv_w.astype(F32), F32, n=2 * kvw)
    kvs = _matmul(xb, kv_w.astype(F32), BF16, col0=2 * kvw, n=4 * kvw)
    cmp_kv = _compress(kvc,
                       jnp.stack([cmp_pos_k, cmp_pos_v]).astype(F32),
                       jnp.stack([cmp_w1_k, cmp_w1_v]).astype(BF16),
                       jnp.stack([cmp_w2_k, cmp_w2_v]).astype(BF16), bsz, seq)

    for i in range(n_b):
        layer = n_a + i
        w_in = b_w_in[i]
        n_main = (1 + N_BRANCH) * width
        w_main = w_in[:, :n_main].astype(BF16)
        w_gate = jnp.pad(w_in[:, n_main:], ((0, 0), (0, LANES - (w_in.shape[1] - n_main)))).astype(BF16)
        q = _matmul(xb, w_main, BF16, n=width, scale=HEAD_DIM ** -0.5)
        z = _matmul(xb, w_main, F32, col0=width, n=N_BRANCH * width)
        gate_logits = _matmul(xb, w_gate, F32)
        o_cmp, sel = _cmp_select(q, cmp_kv, bsz, seq)
        o_slc, o_win = _slc_win(q, sel, kvs, bsz, seq)
        xf, xb = _nsa_post(o_cmp, o_slc, o_win, z, gate_logits, xf, b_w_out[i].astype(BF16),
                           ln_g[layer].reshape(1, dm).astype(F32), ln_b[layer].reshape(1, dm).astype(F32), alpha)

    return xf.reshape(bsz, seq, dm).astype(x.dtype)
```

```python
import functools
import math

import jax
import jax.numpy as jnp
from jax import lax
from jax.experimental import pallas as pl
from jax.experimental.pallas import tpu as pltpu

F32 = jnp.float32
BF16 = jnp.bfloat16

HEAD_DIM = 128
N_HEADS = 16
N_KV_HEADS = 4
HEADS_PER_KV = N_HEADS // N_KV_HEADS
GROUP_WIDTH = HEADS_PER_KV * HEAD_DIM
N_BRANCH = 3
CMP_BLOCK = 32
CMP_STRIDE = 16
SEL_BLOCK = 64
N_SELECT = 16
WINDOW = 512
SSM_GROUP = 16
SSM_STATE = 64
SSM_CHUNK_GROUPS = 16
SSM_CHUNK_IN = SSM_CHUNK_GROUPS * SSM_GROUP
SSM_CHUNK_STATE = SSM_CHUNK_GROUPS * SSM_STATE
LN_EPS = 1e-5
MASK_VALUE = -1e30
FORCE_SCORE = 1e6
PAD_SCORE = -3e38
LOG2_E = math.log2(math.e)

SUBLANES = 8
LANES = 128
VMEM_LIMIT_BYTES = 48 * 1024 * 1024

SCAN_T = 256
CMP_TQ = 512
ATT_TQ = 512
ATT_TK = 512
WIN_TQ = 128
ATT_VMEM_LIMIT_BYTES = 58 * 1024 * 1024


def _params(*sem):
    return pltpu.CompilerParams(dimension_semantics=sem, vmem_limit_bytes=VMEM_LIMIT_BYTES)


def _sigmoid(v):
    return 1.0 / (1.0 + jnp.exp(-v))


def _layer_norm(v, g, b):
    mu = jnp.mean(v, axis=-1, keepdims=True)
    d = v - mu
    var = jnp.mean(d * d, axis=-1, keepdims=True)
    return d * lax.rsqrt(var + LN_EPS) * g + b


def _mm_kernel(a_ref, w_ref, o_ref, *wb_refs, scale, lane_blocked, transposed):
    if wb_refs:
        wb_ref, = wb_refs

        @pl.when(pl.program_id(1) == 0)
        def _():
            wb_ref[...] = w_ref[...].astype(BF16)

        w_ref = wb_ref
    contract = (((1,), (1 if transposed else 0,)), ((), ()))
    acc = lax.dot_general(a_ref[...], w_ref[...], contract, preferred_element_type=F32)
    if scale is not None:
        acc = acc * scale
    if lane_blocked:
        for j in range(o_ref.shape[0]):
            o_ref[j] = acc[:, j * LANES:(j + 1) * LANES].astype(o_ref.dtype)
    else:
        o_ref[...] = acc.astype(o_ref.dtype)


def _matmul(a, w, out_dtype, *, n=None, col0=0, layer=None, transposed=False, tm=512, tn=1024, scale=None,
            lane_blocked=False):
    m, k = a.shape
    n = w.shape[-2 if transposed else -1] if n is None else n
    tm = min(tm, m)
    tn = min(tn, n)
    assert m % tm == 0 and n % tn == 0 and col0 % tn == 0
    jb = col0 // tn
    w_block = (tn, k) if transposed else (k, tn)
    w_index = (lambda j: (j + jb, 0)) if transposed else (lambda j: (0, j + jb))
    if layer is None:
        w_spec = pl.BlockSpec(w_block, lambda j, i: w_index(j))
    else:
        w_spec = pl.BlockSpec((None,) + w_block, lambda j, i: (layer,) + w_index(j))
    if lane_blocked:
        out_spec = pl.BlockSpec((tn // LANES, tm, LANES), lambda j, i: (j, i, 0))
        out_shape = jax.ShapeDtypeStruct((n // LANES, m, LANES), out_dtype)
    else:
        out_spec = pl.BlockSpec((tm, tn), lambda j, i: (i, j))
        out_shape = jax.ShapeDtypeStruct((m, n), out_dtype)
    return pl.pallas_call(
        functools.partial(_mm_kernel, scale=scale, lane_blocked=lane_blocked, transposed=transposed),
        grid=(n // tn, m // tm),
        in_specs=[pl.BlockSpec((tm, k), lambda j, i: (i, 0)), w_spec],
        out_specs=out_spec,
        out_shape=out_shape,
        scratch_shapes=[] if w.dtype == BF16 else [pltpu.VMEM(w_block, BF16)],
        compiler_params=_params("parallel", "arbitrary"),
        name="matmul",
    )(a, w)


def _s5_scan_kernel(u_ref, wb_ref, cb_ref, a_ref, o_ref, bu_ref, hb_ref, carry_ref, *, t_tok):
    s_len = t_tok // SUBLANES
    p = SSM_CHUNK_STATE

    @pl.when(pl.program_id(2) == 0)
    def _():
        carry_ref[...] = jnp.zeros_like(carry_ref)
        hb_ref[...] = jnp.zeros_like(hb_ref)

    n_lb = SSM_CHUNK_IN // LANES
    rows = [jnp.concatenate([u_ref[c, pl.ds(k, SUBLANES, stride=s_len), :] for c in range(n_lb)], axis=1)
            for k in range(s_len)]
    up = jnp.concatenate(rows, axis=0).astype(BF16)
    bu_ref[...] = jnp.dot(up, wb_ref[...], preferred_element_type=F32)

    y = lax.dot_general(hb_ref[...].astype(BF16), cb_ref[...], (((1,), (1,)), ((), ())),
                        preferred_element_type=F32)
    for k in range(s_len):
        for c in range(n_lb):
            o_ref[c, pl.ds(k, SUBLANES, stride=s_len), :] = y[k * SUBLANES:(k + 1) * SUBLANES,
                                                               c * LANES:(c + 1) * LANES]

    sub = lax.broadcasted_iota(jnp.int32, (SUBLANES, LANES), 0)
    for blk in range(p // LANES):
        re = slice(2 * blk * LANES, (2 * blk + 1) * LANES)
        im = slice((2 * blk + 1) * LANES, (2 * blk + 2) * LANES)
        st = slice(blk * LANES, (blk + 1) * LANES)
        ar = jnp.broadcast_to(a_ref[0:1, st], (SUBLANES, LANES))
        ai = jnp.broadcast_to(a_ref[1:2, st], (SUBLANES, LANES))
        sr = jnp.broadcast_to(a_ref[2:3, st], (SUBLANES, LANES))
        si = jnp.broadcast_to(a_ref[3:4, st], (SUBLANES, LANES))

        def step(k, hr, hi):
            br = bu_ref[k * SUBLANES:(k + 1) * SUBLANES, re]
            bi = bu_ref[k * SUBLANES:(k + 1) * SUBLANES, im]
            return ar * hr - ai * hi + br, ar * hi + ai * hr + bi

        er = jnp.zeros((SUBLANES, LANES), F32)
        ei = jnp.zeros((SUBLANES, LANES), F32)
        for k in range(s_len):
            er, ei = step(k, er, ei)

        hinr = jnp.where(sub == 0, carry_ref[:, re], 0.0)
        hini = jnp.where(sub == 0, carry_ref[:, im], 0.0)
        for j in range(SUBLANES - 1):
            nr = sr * hinr - si * hini + er
            ni = sr * hini + si * hinr + ei
            hinr = jnp.where(sub == j + 1, pltpu.roll(nr, 1, 0), hinr)
            hini = jnp.where(sub == j + 1, pltpu.roll(ni, 1, 0), hini)
        nr = sr * hinr - si * hini + er
        ni = sr * hini + si * hinr + ei
        carry_ref[:, re] = jnp.broadcast_to(nr[SUBLANES - 1:SUBLANES, :], (SUBLANES, LANES))
        carry_ref[:, im] = jnp.broadcast_to(ni[SUBLANES - 1:SUBLANES, :], (SUBLANES, LANES))

        hr, hi = hinr, hini
        for k in range(s_len):
            hr, hi = step(k, hr, hi)
            hb_ref[k * SUBLANES:(k + 1) * SUBLANES, re] = hr
            hb_ref[k * SUBLANES:(k + 1) * SUBLANES, im] = hi


def _s5_scan(uz, wb, cb, acoef, bsz, seq):
    n_chunks = wb.shape[0]
    e = n_chunks * SSM_CHUNK_IN
    n_lb = SSM_CHUNK_IN // LANES
    t_tok = min(SCAN_T, seq)
    nt = seq // t_tok
    assert seq % t_tok == 0 and t_tok % (SUBLANES * SUBLANES) == 0
    return pl.pallas_call(
        functools.partial(_s5_scan_kernel, t_tok=t_tok),
        grid=(bsz, n_chunks, nt + 1),
        in_specs=[
            pl.BlockSpec((n_lb, t_tok, LANES), lambda b, c, t: (c, b * nt + jnp.minimum(t, nt - 1), 0)),
            pl.BlockSpec((None, SSM_CHUNK_IN, 2 * SSM_CHUNK_STATE), lambda b, c, t: (c, 0, 0)),
            pl.BlockSpec((None, SSM_CHUNK_IN, 2 * SSM_CHUNK_STATE), lambda b, c, t: (c, 0, 0)),
            pl.BlockSpec((None, SUBLANES, SSM_CHUNK_STATE), lambda b, c, t: (c, 0, 0)),
        ],
        out_specs=pl.BlockSpec((n_lb, t_tok, LANES), lambda b, c, t: (c, b * nt + jnp.maximum(t - 1, 0), 0)),
        out_shape=jax.ShapeDtypeStruct((e // LANES, bsz * seq, LANES), F32),
        scratch_shapes=[pltpu.VMEM((t_tok, 2 * SSM_CHUNK_STATE), F32),
                        pltpu.VMEM((t_tok, 2 * SSM_CHUNK_STATE), F32),
                        pltpu.VMEM((SUBLANES, 2 * SSM_CHUNK_STATE), F32)],
        compiler_params=_params("parallel", "parallel", "arbitrary"),
        name="s5_scan",
    )(uz, wb, cb, acoef)


def _s5_constants(lam_re, lam_im, log_dt, b_re, b_im, c_re, c_im, s_len):
    lr = lam_re.astype(F32)
    li = lam_im.astype(F32)
    dt = jnp.exp(log_dt.astype(F32))[:, None]
    mag = jnp.exp(lr * dt)
    ar = mag * jnp.cos(li * dt)
    ai = mag * jnp.sin(li * dt)
    inv_abs2 = 1.0 / (lr * lr + li * li)
    cr = ((ar - 1.0) * lr + ai * li) * inv_abs2
    ci = (ai * lr - (ar - 1.0) * li) * inv_abs2
    br = b_re.astype(F32)
    bi = b_im.astype(F32)
    bbar_r = cr[..., None] * br - ci[..., None] * bi
    bbar_i = cr[..., None] * bi + ci[..., None] * br
    g = lr.shape[0]
    nck = g // SSM_CHUNK_GROUPS
    n_cols = 2 * SSM_CHUNK_STATE
    col = jnp.arange(n_cols)
    col_group = 2 * (col // (2 * LANES)) + (col % LANES) // SSM_STATE
    on_diagonal = (jnp.arange(SSM_CHUNK_GROUPS)[:, None] == col_group[None, :])[None, :, None, :]

    def block_diagonal(w_re, w_im):
        pair = jnp.concatenate([w_re, w_re, w_im, w_im], axis=-1)
        row = jnp.tile(pair, (1, 1, n_cols // pair.shape[-1]))
        row = row.reshape(nck, SSM_CHUNK_GROUPS, SSM_GROUP, n_cols)
        return jnp.where(on_diagonal, row, 0.0).reshape(nck, SSM_CHUNK_IN, n_cols).astype(BF16)

    wb = block_diagonal(jnp.swapaxes(bbar_r, 1, 2), jnp.swapaxes(bbar_i, 1, 2))
    cbt = block_diagonal(c_re.astype(F32), -c_im.astype(F32))
    sr, si = ar, ai
    for _ in range(int(math.log2(s_len))):
        sr, si = sr * sr - si * si, 2.0 * sr * si
    rows = [v.reshape(nck, 1, SSM_CHUNK_STATE) for v in (ar, ai, sr, si)]
    rows += [jnp.zeros_like(rows[0])] * (SUBLANES - len(rows))
    return wb, cbt, jnp.concatenate(rows, axis=1)


def _s5_post_kernel(y_ref, u_ref, z_ref, x_ref, d_ref, wg_ref, bg_ref, wo_ref, lg_ref, lb_ref,
                    xo_ref, xb_ref, *, alpha):
    unblock = lambda ref: jnp.concatenate([ref[j] for j in range(ref.shape[0])], axis=1)
    y = unblock(y_ref) + d_ref[...] * unblock(u_ref)
    g = jax.nn.gelu(y)
    gl = jnp.dot(g.astype(BF16), wg_ref[...], preferred_element_type=F32) + bg_ref[...]
    y2 = g * _sigmoid(gl)
    z = unblock(z_ref)
    a = (y2 * (z * _sigmoid(z))).astype(BF16)
    o = jnp.dot(a, wo_ref[...], preferred_element_type=F32)
    xn = _layer_norm(alpha * x_ref[...] + o, lg_ref[...], lb_ref[...])
    xo_ref[...] = xn
    xb_ref[...] = xn.astype(BF16)


def _s5_post(yssm, uz, x, d, w_glu, b_glu, w_out, ln_g, ln_b, alpha, tm=256):
    nlb, m, _ = yssm.shape
    e = nlb * LANES
    dm = x.shape[1]
    tm = min(tm, m)
    row = lambda i: (i, 0)
    fixed = lambda i: (0, 0)
    return pl.pallas_call(
        functools.partial(_s5_post_kernel, alpha=alpha),
        grid=(m // tm,),
        in_specs=[pl.BlockSpec((nlb, tm, LANES), lambda i: (0, i, 0)),
                  pl.BlockSpec((nlb, tm, LANES), lambda i: (0, i, 0)),
                  pl.BlockSpec((nlb, tm, LANES), lambda i: (1, i, 0)),
                  pl.BlockSpec((tm, dm), row),
                  pl.BlockSpec((1, e), fixed),
                  pl.BlockSpec((e, e), fixed),
                  pl.BlockSpec((1, e), fixed),
                  pl.BlockSpec((e, dm), fixed),
                  pl.BlockSpec((1, dm), fixed),
                  pl.BlockSpec((1, dm), fixed)],
        out_specs=[pl.BlockSpec((tm, dm), row), pl.BlockSpec((tm, dm), row)],
        out_shape=[jax.ShapeDtypeStruct((m, dm), F32), jax.ShapeDtypeStruct((m, dm), BF16)],
        compiler_params=_params("parallel"),
        name="s5_post",
    )(yssm, uz, uz, x, d, w_glu, b_glu, w_out, ln_g, ln_b)


def _compress_kernel(t_ref, pos_ref, w1_ref, w2_ref, o_ref, q_ref, *, nc):
    half = CMP_BLOCK // 2
    acc_p = jnp.zeros((nc, HEAD_DIM), F32)
    acc_q = jnp.zeros((nc, HEAD_DIM), F32)
    for l in range(half):
        t = t_ref[pl.ds(l, nc, stride=CMP_STRIDE), :]
        lo = (t + pos_ref[l:l + 1, :]).astype(BF16)
        hi = (t + pos_ref[half + l:half + l + 1, :]).astype(BF16)
        acc_p += jnp.dot(lo, w1_ref[l * HEAD_DIM:(l + 1) * HEAD_DIM, :], preferred_element_type=F32)
        acc_q += jnp.dot(hi, w1_ref[(half + l) * HEAD_DIM:(half + l + 1) * HEAD_DIM, :],
                         preferred_element_type=F32)
    q_ref[0:nc, :] = acc_q
    q_ref[nc:nc + SUBLANES, :] = jnp.zeros((SUBLANES, HEAD_DIM), F32)
    pre = acc_p + q_ref[pl.ds(1, nc), :]
    mid = jax.nn.gelu(pre).astype(BF16)
    o_ref[...] = jnp.dot(mid, w2_ref[...], preferred_element_type=F32).astype(o_ref.dtype)


def _compress(kvc, pos, w1, w2, bsz, seq):
    nc = seq // CMP_STRIDE
    g = N_KV_HEADS
    return pl.pallas_call(
        functools.partial(_compress_kernel, nc=nc),
        grid=(bsz, 2, g),
        in_specs=[pl.BlockSpec((seq, HEAD_DIM), lambda b, s, h: (b, s * g + h)),
                  pl.BlockSpec((None, CMP_BLOCK, HEAD_DIM), lambda b, s, h: (s, 0, 0)),
                  pl.BlockSpec((None, CMP_BLOCK * HEAD_DIM, HEAD_DIM), lambda b, s, h: (s, 0, 0)),
                  pl.BlockSpec((None, HEAD_DIM, HEAD_DIM), lambda b, s, h: (s, 0, 0))],
        out_specs=pl.BlockSpec((None, None, None, nc, HEAD_DIM), lambda b, s, h: (b, s, h, 0, 0)),
        out_shape=jax.ShapeDtypeStruct((bsz, 2, g, nc, HEAD_DIM), BF16),
        scratch_shapes=[pltpu.VMEM((nc + SUBLANES, HEAD_DIM), F32)],
        compiler_params=_params("parallel", "parallel", "parallel"),
        name="kv_compress",
    )(kvc, pos, w1, w2)


def _cmp_select_kernel(q_ref, k_ref, v_ref, o_ref, sel_ref, *, tq, nc, ns, n_sel):
    hp = HEADS_PER_KV
    step = pl.program_id(2)
    q0 = step * tq
    q4 = jnp.concatenate([q_ref[:, h * HEAD_DIM:(h + 1) * HEAD_DIM] for h in range(hp)], axis=0)
    t = q0 + lax.broadcasted_iota(jnp.int32, (tq, nc), 0)
    n = lax.broadcasted_iota(jnp.int32, (tq, nc), 1)
    valid = (n * CMP_STRIDE + (CMP_BLOCK - 1) <= t)[None]
    s = lax.dot_general(q4, k_ref[...], (((1,), (1,)), ((), ())), preferred_element_type=F32)
    s = jnp.where(valid, s.reshape(hp, tq, nc), MASK_VALUE)
    m = jnp.max(s, axis=-1, keepdims=True)
    e = jnp.where(valid, jnp.exp(s - m), 0.0)
    p = e * (1.0 / jnp.maximum(jnp.sum(e, axis=-1, keepdims=True), 1e-30))
    o = jnp.dot(p.reshape(hp * tq, nc).astype(BF16), v_ref[...], preferred_element_type=F32)
    for h in range(hp):
        o_ref[:, h * HEAD_DIM:(h + 1) * HEAD_DIM] = o[h * tq:(h + 1) * tq, :]
    psum = p[0]
    for h in range(1, hp):
        psum = psum + p[h]

    nn = lax.broadcasted_iota(jnp.int32, (nc, LANES), 0)
    jj = lax.broadcasted_iota(jnp.int32, (nc, LANES), 1)
    overlap = ((nn * CMP_STRIDE < (jj + 1) * SEL_BLOCK)
               & (nn * CMP_STRIDE + (CMP_BLOCK - 1) >= jj * SEL_BLOCK)
               & (jj < ns)).astype(BF16)
    p_hi = psum.astype(BF16)
    p_lo = (psum - p_hi.astype(F32)).astype(BF16)
    imp = (jnp.dot(p_hi, overlap, preferred_element_type=F32)
           + jnp.dot(p_lo, overlap, preferred_element_type=F32))

    tt = q0 + lax.broadcasted_iota(jnp.int32, (tq, LANES), 0)
    j = lax.broadcasted_iota(jnp.int32, (tq, LANES), 1)
    cur = tt // SEL_BLOCK
    forced = (j == 0) | (j == cur) | (j == cur - 1)
    imp = jnp.where(j > cur, MASK_VALUE, jnp.where(forced, FORCE_SCORE, imp))
    imp = jnp.where(j < ns, imp, PAD_SCORE)

    x = imp.T
    sub = lax.broadcasted_iota(jnp.int32, (SUBLANES, tq), 0)

    def rank_and_store(live):
        nblk = live // SUBLANES
        if live <= n_sel:
            picked = [jnp.ones((SUBLANES, tq), F32)] * nblk
        else:
            xs = [x[b * SUBLANES:(b + 1) * SUBLANES, :] for b in range(nblk)]
            cnts = [jnp.zeros((SUBLANES, tq), jnp.int32) for _ in range(nblk)]
            for i in range(live):
                xi = x[i:i + 1, :]
                for b in range(nblk):
                    if b * SUBLANES > i:
                        beats = xi >= xs[b]
                    elif (b + 1) * SUBLANES - 1 <= i:
                        beats = xi > xs[b]
                    else:
                        beats = (xi > xs[b]) | ((xi >= xs[b]) & (sub > i - b * SUBLANES))
                    cnts[b] = cnts[b] + beats.astype(jnp.int32)
            picked = [(c < n_sel).astype(F32) for c in cnts]
        sel_t = jnp.concatenate(picked + [jnp.zeros((LANES - live, tq), F32)] * (live < LANES), axis=0)
        sel_ref[...] = sel_t.T.astype(sel_ref.dtype)

    for c in range(ns * SEL_BLOCK // tq):
        pl.when(step == c)(functools.partial(rank_and_store, (c + 1) * tq // SEL_BLOCK))


def _cmp_select(q, cmp_kv, bsz, seq):
    tq = min(CMP_TQ, seq)
    nq = seq // tq
    nc = seq // CMP_STRIDE
    ns = seq // SEL_BLOCK
    assert ns <= LANES and tq % LANES == 0 and tq % (SUBLANES * SEL_BLOCK) == 0
    g = N_KV_HEADS
    return pl.pallas_call(
        functools.partial(_cmp_select_kernel, tq=tq, nc=nc, ns=ns, n_sel=min(N_SELECT, ns)),
        grid=(bsz, g, nq),
        in_specs=[pl.BlockSpec((tq, GROUP_WIDTH), lambda b, h, i: (b * nq + i, h)),
                  pl.BlockSpec((None, None, None, nc, HEAD_DIM), lambda b, h, i: (b, 0, h, 0, 0)),
                  pl.BlockSpec((None, None, None, nc, HEAD_DIM), lambda b, h, i: (b, 1, h, 0, 0))],
        out_specs=[pl.BlockSpec((tq, GROUP_WIDTH), lambda b, h, i: (b * nq + i, h)),
                   pl.BlockSpec((None, None, tq, LANES), lambda b, h, i: (b, h, i, 0))],
        out_shape=[jax.ShapeDtypeStruct((bsz * seq, N_HEADS * HEAD_DIM), F32),
                   jax.ShapeDtypeStruct((bsz, g, seq, LANES), BF16)],
        compiler_params=_params("parallel", "parallel", "parallel"),
        name="nsa_cmp_select",
    )(q, cmp_kv, cmp_kv)


def _lane_chunks(v):
    return [v[:, c * LANES:(c + 1) * LANES] for c in range(v.shape[1] // LANES)]


def _softmax_pv(chunks, v):
    mx = chunks[0]
    for sc in chunks[1:]:
        mx = jnp.maximum(mx, sc)
    m = jnp.broadcast_to(jnp.max(mx, axis=-1, keepdims=True), mx.shape)
    es = [jnp.exp2(sc - m) for sc in chunks]
    ls = es[0]
    for e in es[1:]:
        ls = ls + e
    e = jnp.concatenate([e.astype(BF16) for e in es], axis=1)
    return jnp.dot(e, v, preferred_element_type=F32) / jnp.sum(ls, axis=-1, keepdims=True)


def _slc_win_kernel(q_ref, sel_ref, oh_ref, ks_ref, vs_ref, kw_ref, vw_ref, os_ref, ow_ref, s_ref, *,
                    tq, tk, seq):
    hp = HEADS_PER_KV
    rows = hp * tq
    q0 = pl.program_id(2) * tq
    q4 = jnp.concatenate([q_ref[:, h * HEAD_DIM:(h + 1) * HEAD_DIM] for h in range(hp)], axis=0)
    neg = ((sel_ref[...].astype(F32) - 1.0) * (-MASK_VALUE)).astype(BF16)
    q_aug = jnp.concatenate([q4, jnp.concatenate([neg] * hp, axis=0)], axis=1)
    nt_dims = (((1,), (1,)), ((), ()))
    n_tiles = (q0 + tq - 1) // tk + 1
    last = n_tiles - 1

    def masked_scores(kt):
        k0 = pl.multiple_of(kt * tk, tk)
        k_aug = jnp.concatenate([ks_ref[pl.ds(k0, tk), :], oh_ref[pl.ds(k0, tk), :]], axis=1)
        s = lax.dot_general(q_aug, k_aug, nt_dims, preferred_element_type=F32)
        return s * LOG2_E

    def lane_max(s, mx):
        for sc in _lane_chunks(s):
            mx = jnp.maximum(mx, sc)
        return mx

    def scores(kt, mx):
        s = masked_scores(kt)
        s_ref[kt] = s
        return lane_max(s, mx)

    mx = lax.fori_loop(0, last, scores, jnp.full((rows, LANES), MASK_VALUE, F32))
    t = q0 + lax.broadcasted_iota(jnp.int32, (tq, tk), 0)
    kcol = last * tk + lax.broadcasted_iota(jnp.int32, (tq, tk), 1)
    s = jnp.where((kcol <= t)[None], masked_scores(last).reshape(hp, tq, tk), MASK_VALUE).reshape(rows, tk)
    s_ref[last] = s
    mx = lane_max(s, mx)
    m = jnp.broadcast_to(jnp.max(mx, axis=-1, keepdims=True), (rows, LANES))

    def weighted(kt, carry):
        ls, acc = carry
        k0 = pl.multiple_of(kt * tk, tk)
        es = [jnp.exp2(sc - m) for sc in _lane_chunks(s_ref[kt])]
        for e in es:
            ls = ls + e
        e = jnp.concatenate([e.astype(BF16) for e in es], axis=1)
        return ls, acc + jnp.dot(e, vs_ref[pl.ds(k0, tk), :], preferred_element_type=F32)

    ls, acc = lax.fori_loop(0, n_tiles, weighted,
                            (jnp.zeros((rows, LANES), F32), jnp.zeros((rows, HEAD_DIM), F32)))
    o = acc / jnp.sum(ls, axis=-1, keepdims=True)
    for h in range(hp):
        os_ref[:, h * HEAD_DIM:(h + 1) * HEAD_DIM] = o[h * tq:(h + 1) * tq, :]

    tw = min(WIN_TQ, tq)
    span = min(WINDOW + tw, seq)
    for j in range(tq // tw):
        qw0 = q0 + j * tw
        start = pl.multiple_of(jnp.clip(qw0 + tw - span, 0, seq - span), tw)
        kpos = start + lax.broadcasted_iota(jnp.int32, (tw, span), 1)
        tpos = qw0 + lax.broadcasted_iota(jnp.int32, (tw, span), 0)
        wbias = jnp.where((kpos <= tpos) & (kpos > tpos - WINDOW), 0.0, MASK_VALUE)
        qw = jnp.concatenate([q_ref[j * tw:(j + 1) * tw, h * HEAD_DIM:(h + 1) * HEAD_DIM] for h in range(hp)],
                             axis=0)
        sw = lax.dot_general(qw, kw_ref[pl.ds(start, span), :], nt_dims, preferred_element_type=F32) * LOG2_E
        sw = (sw.reshape(hp, tw, span) + wbias[None]).reshape(hp * tw, span)
        ow = _softmax_pv(_lane_chunks(sw), vw_ref[pl.ds(start, span), :])
        for h in range(hp):
            ow_ref[j * tw:(j + 1) * tw, h * HEAD_DIM:(h + 1) * HEAD_DIM] = ow[h * tw:(h + 1) * tw, :]


def _slc_win(q, sel, kvs, bsz, seq):
    tq = min(ATT_TQ, seq)
    tk = min(ATT_TK, seq)
    nq = seq // tq
    g = N_KV_HEADS
    block_onehot = (jnp.arange(seq)[:, None] // SEL_BLOCK == jnp.arange(LANES)[None, :]).astype(BF16)
    kv_spec = lambda part: pl.BlockSpec((seq, HEAD_DIM), lambda b, h, i: (b, part * g + h))
    out_spec = pl.BlockSpec((tq, GROUP_WIDTH), lambda b, h, i: (b * nq + i, h))
    out_sds = jax.ShapeDtypeStruct((bsz * seq, N_HEADS * HEAD_DIM), F32)
    return pl.pallas_call(
        functools.partial(_slc_win_kernel, tq=tq, tk=tk, seq=seq),
        grid=(bsz, g, nq),
        in_specs=[pl.BlockSpec((tq, GROUP_WIDTH), lambda b, h, i: (b * nq + i, h)),
                  pl.BlockSpec((None, None, tq, LANES), lambda b, h, i: (b, h, i, 0)),
                  pl.BlockSpec((seq, LANES), lambda b, h, i: (0, 0)),
                  kv_spec(0), kv_spec(1), kv_spec(2), kv_spec(3)],
        out_specs=[out_spec, out_spec],
        out_shape=[out_sds, out_sds],
        scratch_shapes=[pltpu.VMEM((seq // tk, HEADS_PER_KV * tq, tk), F32)],
        compiler_params=pltpu.CompilerParams(dimension_semantics=("parallel", "parallel", "parallel"),
                                             vmem_limit_bytes=ATT_VMEM_LIMIT_BYTES),
        name="nsa_slc_win",
    )(q, sel, block_onehot, kvs, kvs, kvs, kvs)


def _nsa_post_kernel(oc_ref, os_ref, ow_ref, z_ref, gl_ref, x_ref, wo_ref, lg_ref, lb_ref,
                     xo_ref, xb_ref, *, alpha):
    width = N_HEADS * HEAD_DIM
    gate = _sigmoid(gl_ref[...])
    parts = []
    for h in range(N_HEADS):
        hs = slice(h * HEAD_DIM, (h + 1) * HEAD_DIM)
        acc = None
        for br, o_ref in enumerate((oc_ref, os_ref, ow_ref)):
            gcol = gate[:, br * N_HEADS + h:br * N_HEADS + h + 1]
            z = z_ref[:, br * width + h * HEAD_DIM:br * width + (h + 1) * HEAD_DIM]
            term = gcol * o_ref[:, hs] * (z * _sigmoid(z))
            acc = term if acc is None else acc + term
        parts.append(acc.astype(BF16))
    y = jnp.concatenate(parts, axis=1)
    o = jnp.dot(y, wo_ref[...], preferred_element_type=F32)
    xn = _layer_norm(alpha * x_ref[...] + o, lg_ref[...], lb_ref[...])
    xo_ref[...] = xn
    xb_ref[...] = xn.astype(BF16)


def _nsa_post(o_cmp, o_slc, o_win, z, gate_logits, x, w_out, ln_g, ln_b, alpha, tm=256):
    m, dm = x.shape
    width = N_HEADS * HEAD_DIM
    tm = min(tm, m)
    row = lambda i: (i, 0)
    fixed = lambda i: (0, 0)
    return pl.pallas_call(
        functools.partial(_nsa_post_kernel, alpha=alpha),
        grid=(m // tm,),
        in_specs=[pl.BlockSpec((tm, width), row),
                  pl.BlockSpec((tm, width), row),
                  pl.BlockSpec((tm, width), row),
                  pl.BlockSpec((tm, N_BRANCH * width), row),
                  pl.BlockSpec((tm, LANES), row),
                  pl.BlockSpec((tm, dm), row),
                  pl.BlockSpec((width, dm), fixed, pipeline_mode=pl.Buffered(1)),
                  pl.BlockSpec((1, dm), fixed),
                  pl.BlockSpec((1, dm), fixed)],
        out_specs=[pl.BlockSpec((tm, dm), row), pl.BlockSpec((tm, dm), row)],
        out_shape=[jax.ShapeDtypeStruct((m, dm), F32), jax.ShapeDtypeStruct((m, dm), BF16)],
        compiler_params=_params("parallel"),
        name="nsa_post",
    )(o_cmp, o_slc, o_win, z, gate_logits, x, w_out, ln_g, ln_b)


def kernel(x, a_w_in, a_lam_re, a_lam_im, a_log_dt, a_b_re, a_b_im, a_c_re, a_c_im, a_d, a_w_glu, a_b_glu, a_w_out, kv_w, cmp_pos_k, cmp_w1_k, cmp_w2_k, cmp_pos_v, cmp_w1_v, cmp_w2_v, b_w_in, b_w_out, ln_g, ln_b):
    bsz, seq, dm = x.shape
    n_a = a_w_in.shape[0]
    n_b = b_w_in.shape[0]
    depth = n_a + n_b
    alpha = (2 * depth) ** 0.25
    width = N_HEADS * HEAD_DIM
    kvw = N_KV_HEADS * HEAD_DIM
    s_len = min(SCAN_T, seq) // SUBLANES

    xf = x.reshape(bsz * seq, dm).astype(F32)
    xb = xf.astype(BF16)

    for i in range(n_a):
        wb, cb, acoef = _s5_constants(a_lam_re[i], a_lam_im[i], a_log_dt[i], a_b_re[i], a_b_im[i],
                                      a_c_re[i], a_c_im[i], s_len)
        e = a_w_glu.shape[1]
        uz = _matmul(xb, a_w_in.astype(F32), F32, layer=i, lane_blocked=True)
        yssm = _s5_scan(uz, wb, cb, acoef, bsz, seq)
        xf, xb = _s5_post(yssm, uz, xf, a_d[i].reshape(1, e).astype(F32), a_w_glu[i].astype(BF16),
                          a_b_glu[i].reshape(1, e).astype(F32), a_w_out[i].astype(BF16),
                          ln_g[i].reshape(1, dm).astype(F32), ln_b[i].reshape(1, dm).astype(F32), alpha)

    kvc = _matmul(xb, kv_w.astype(F32), F32, n=2 * kvw)
    kvs = _matmul(xb, kv_w.astype(F32), BF16, col0=2 * kvw, n=4 * kvw)
    cmp_kv = _compress(kvc,
                       jnp.stack([cmp_pos_k, cmp_pos_v]).astype(F32),
                       jnp.stack([cmp_w1_k, cmp_w1_v]).astype(BF16),
                       jnp.stack([cmp_w2_k, cmp_w2_v]).astype(BF16), bsz, seq)

    b_w_in_t = jnp.swapaxes(b_w_in, 1, 2).astype(F32)
    n_main = (1 + N_BRANCH) * width
    for i in range(n_b):
        layer = n_a + i
        w_gate_t = b_w_in_t[i, n_main:, :]
        w_gate_t = jnp.pad(w_gate_t, ((0, LANES - w_gate_t.shape[0]), (0, 0))).astype(BF16)
        q = _matmul(xb, b_w_in_t, BF16, layer=i, transposed=True, n=width, scale=HEAD_DIM ** -0.5)
        z = _matmul(xb, b_w_in_t, F32, layer=i, transposed=True, col0=width, n=N_BRANCH * width)
        gate_logits = _matmul(xb, w_gate_t, F32, transposed=True)
        o_cmp, sel = _cmp_select(q, cmp_kv, bsz, seq)
        o_slc, o_win = _slc_win(q, sel, kvs, bsz, seq)
        xf, xb = _nsa_post(o_cmp, o_slc, o_win, z, gate_logits, xf, b_w_out[i].astype(BF16),
                           ln_g[layer].reshape(1, dm).astype(F32), ln_b[layer].reshape(1, dm).astype(F32), alpha)

    return xf.reshape(bsz, seq, dm).astype(x.dtype)
```

```python
import functools
import math

import jax
import jax.numpy as jnp
from jax import lax
from jax.experimental import pallas as pl
from jax.experimental.pallas import tpu as pltpu

F32 = jnp.float32
BF16 = jnp.bfloat16

HEAD_DIM = 128
N_HEADS = 16
N_KV_HEADS = 4
HEADS_PER_KV = N_HEADS // N_KV_HEADS
GROUP_WIDTH = HEADS_PER_KV * HEAD_DIM
N_BRANCH = 3
CMP_BLOCK = 32
CMP_STRIDE = 16
SEL_BLOCK = 64
N_SELECT = 16
WINDOW = 512
SSM_GROUP = 16
SSM_STATE = 64
SSM_CHUNK_GROUPS = 16
SSM_CHUNK_IN = SSM_CHUNK_GROUPS * SSM_GROUP
SSM_CHUNK_STATE = SSM_CHUNK_GROUPS * SSM_STATE
LN_EPS = 1e-5
MASK_VALUE = -1e30
FORCE_SCORE = 1e6
PAD_SCORE = -3e38
LOG2_E = math.log2(math.e)

SUBLANES = 8
LANES = 128
VMEM_LIMIT_BYTES = 48 * 1024 * 1024

SCAN_T = 256
CMP_TQ = 512
ATT_TQ = 512
ATT_TK = 512
WIN_TQ = 128
ATT_VMEM_LIMIT_BYTES = 58 * 1024 * 1024


def _params(*sem):
    return pltpu.CompilerParams(dimension_semantics=sem, vmem_limit_bytes=VMEM_LIMIT_BYTES)


def _sigmoid(v):
    return 1.0 / (1.0 + jnp.exp(-v))


def _layer_norm(v, g, b):
    mu = jnp.mean(v, axis=-1, keepdims=True)
    d = v - mu
    var = jnp.mean(d * d, axis=-1, keepdims=True)
    return d * lax.rsqrt(var + LN_EPS) * g + b


def _mm_kernel(a_ref, w_ref, o_ref, *wb_refs, scale, silu, lane_blocked, transposed):
    if wb_refs:
        wb_ref, = wb_refs

        @pl.when(pl.program_id(1) == 0)
        def _():
            wb_ref[...] = w_ref[...].astype(BF16)

        w_ref = wb_ref
    contract = (((1,), (1 if transposed else 0,)), ((), ()))
    acc = lax.dot_general(a_ref[...], w_ref[...], contract, preferred_element_type=F32)
    if scale is not None:
        acc = acc * scale
    if silu:
        acc = acc * _sigmoid(acc)
    if lane_blocked:
        for j in range(o_ref.shape[0]):
            o_ref[j] = acc[:, j * LANES:(j + 1) * LANES].astype(o_ref.dtype)
    else:
        o_ref[...] = acc.astype(o_ref.dtype)


def _matmul(a, w, out_dtype, *, n=None, col0=0, layer=None, transposed=False, tm=512, tn=1024, scale=None,
            silu=False, lane_blocked=False):
    m, k = a.shape
    n = w.shape[-2 if transposed else -1] if n is None else n
    tm = min(tm, m)
    tn = min(tn, n)
    assert m % tm == 0 and n % tn == 0 and col0 % tn == 0
    jb = col0 // tn
    w_block = (tn, k) if transposed else (k, tn)
    w_index = (lambda j: (j + jb, 0)) if transposed else (lambda j: (0, j + jb))
    if layer is None:
        w_spec = pl.BlockSpec(w_block, lambda j, i: w_index(j))
    else:
        w_spec = pl.BlockSpec((None,) + w_block, lambda j, i: (layer,) + w_index(j))
    if lane_blocked:
        out_spec = pl.BlockSpec((tn // LANES, tm, LANES), lambda j, i: (j, i, 0))
        out_shape = jax.ShapeDtypeStruct((n // LANES, m, LANES), out_dtype)
    else:
        out_spec = pl.BlockSpec((tm, tn), lambda j, i: (i, j))
        out_shape = jax.ShapeDtypeStruct((m, n), out_dtype)
    return pl.pallas_call(
        functools.partial(_mm_kernel, scale=scale, silu=silu, lane_blocked=lane_blocked, transposed=transposed),
        grid=(n // tn, m // tm),
        in_specs=[pl.BlockSpec((tm, k), lambda j, i: (i, 0)), w_spec],
        out_specs=out_spec,
        out_shape=out_shape,
        scratch_shapes=[] if w.dtype == BF16 else [pltpu.VMEM(w_block, BF16)],
        compiler_params=_params("parallel", "arbitrary"),
        name="matmul",
    )(a, w)


def _s5_scan_kernel(u_ref, wb_ref, cb_ref, a_ref, o_ref, bu_ref, hb_ref, carry_ref, *, t_tok):
    s_len = t_tok // SUBLANES
    p = SSM_CHUNK_STATE

    @pl.when(pl.program_id(2) == 0)
    def _():
        carry_ref[...] = jnp.zeros_like(carry_ref)
        hb_ref[...] = jnp.zeros_like(hb_ref)

    n_lb = SSM_CHUNK_IN // LANES
    rows = [jnp.concatenate([u_ref[c, pl.ds(k, SUBLANES, stride=s_len), :] for c in range(n_lb)], axis=1)
            for k in range(s_len)]
    up = jnp.concatenate(rows, axis=0).astype(BF16)
    bu_ref[...] = jnp.dot(up, wb_ref[...], preferred_element_type=F32)

    y = lax.dot_general(hb_ref[...].astype(BF16), cb_ref[...], (((1,), (1,)), ((), ())),
                        preferred_element_type=F32)
    for k in range(s_len):
        for c in range(n_lb):
            o_ref[c, pl.ds(k, SUBLANES, stride=s_len), :] = y[k * SUBLANES:(k + 1) * SUBLANES,
                                                               c * LANES:(c + 1) * LANES]

    sub = lax.broadcasted_iota(jnp.int32, (SUBLANES, LANES), 0)
    for blk in range(p // LANES):
        re = slice(2 * blk * LANES, (2 * blk + 1) * LANES)
        im = slice((2 * blk + 1) * LANES, (2 * blk + 2) * LANES)
        st = slice(blk * LANES, (blk + 1) * LANES)
        ar = jnp.broadcast_to(a_ref[0:1, st], (SUBLANES, LANES))
        ai = jnp.broadcast_to(a_ref[1:2, st], (SUBLANES, LANES))
        sr = jnp.broadcast_to(a_ref[2:3, st], (SUBLANES, LANES))
        si = jnp.broadcast_to(a_ref[3:4, st], (SUBLANES, LANES))

        def step(k, hr, hi):
            br = bu_ref[k * SUBLANES:(k + 1) * SUBLANES, re]
            bi = bu_ref[k * SUBLANES:(k + 1) * SUBLANES, im]
            return ar * hr - ai * hi + br, ar * hi + ai * hr + bi

        er = jnp.zeros((SUBLANES, LANES), F32)
        ei = jnp.zeros((SUBLANES, LANES), F32)
        for k in range(s_len):
            er, ei = step(k, er, ei)

        hinr = jnp.where(sub == 0, carry_ref[:, re], 0.0)
        hini = jnp.where(sub == 0, carry_ref[:, im], 0.0)
        for j in range(SUBLANES - 1):
            nr = sr * hinr - si * hini + er
            ni = sr * hini + si * hinr + ei
            hinr = jnp.where(sub == j + 1, pltpu.roll(nr, 1, 0), hinr)
            hini = jnp.where(sub == j + 1, pltpu.roll(ni, 1, 0), hini)
        nr = sr * hinr - si * hini + er
        ni = sr * hini + si * hinr + ei
        carry_ref[:, re] = jnp.broadcast_to(nr[SUBLANES - 1:SUBLANES, :], (SUBLANES, LANES))
        carry_ref[:, im] = jnp.broadcast_to(ni[SUBLANES - 1:SUBLANES, :], (SUBLANES, LANES))

        hr, hi = hinr, hini
        for k in range(s_len):
            hr, hi = step(k, hr, hi)
            hb_ref[k * SUBLANES:(k + 1) * SUBLANES, re] = hr
            hb_ref[k * SUBLANES:(k + 1) * SUBLANES, im] = hi


def _s5_scan(uz, wb, cb, acoef, bsz, seq):
    n_chunks = wb.shape[0]
    e = n_chunks * SSM_CHUNK_IN
    n_lb = SSM_CHUNK_IN // LANES
    t_tok = min(SCAN_T, seq)
    nt = seq // t_tok
    assert seq % t_tok == 0 and t_tok % (SUBLANES * SUBLANES) == 0
    return pl.pallas_call(
        functools.partial(_s5_scan_kernel, t_tok=t_tok),
        grid=(bsz, n_chunks, nt + 1),
        in_specs=[
            pl.BlockSpec((n_lb, t_tok, LANES), lambda b, c, t: (c, b * nt + jnp.minimum(t, nt - 1), 0)),
            pl.BlockSpec((None, SSM_CHUNK_IN, 2 * SSM_CHUNK_STATE), lambda b, c, t: (c, 0, 0)),
            pl.BlockSpec((None, SSM_CHUNK_IN, 2 * SSM_CHUNK_STATE), lambda b, c, t: (c, 0, 0)),
            pl.BlockSpec((None, SUBLANES, SSM_CHUNK_STATE), lambda b, c, t: (c, 0, 0)),
        ],
        out_specs=pl.BlockSpec((n_lb, t_tok, LANES), lambda b, c, t: (c, b * nt + jnp.maximum(t - 1, 0), 0)),
        out_shape=jax.ShapeDtypeStruct((e // LANES, bsz * seq, LANES), F32),
        scratch_shapes=[pltpu.VMEM((t_tok, 2 * SSM_CHUNK_STATE), F32),
                        pltpu.VMEM((t_tok, 2 * SSM_CHUNK_STATE), F32),
                        pltpu.VMEM((SUBLANES, 2 * SSM_CHUNK_STATE), F32)],
        compiler_params=_params("parallel", "parallel", "arbitrary"),
        name="s5_scan",
    )(uz, wb, cb, acoef)


def _s5_constants(lam_re, lam_im, log_dt, b_re, b_im, c_re, c_im, s_len):
    lr = lam_re.astype(F32)
    li = lam_im.astype(F32)
    dt = jnp.exp(log_dt.astype(F32))[:, None]
    mag = jnp.exp(lr * dt)
    ar = mag * jnp.cos(li * dt)
    ai = mag * jnp.sin(li * dt)
    inv_abs2 = 1.0 / (lr * lr + li * li)
    cr = ((ar - 1.0) * lr + ai * li) * inv_abs2
    ci = (ai * lr - (ar - 1.0) * li) * inv_abs2
    br = b_re.astype(F32)
    bi = b_im.astype(F32)
    bbar_r = cr[..., None] * br - ci[..., None] * bi
    bbar_i = cr[..., None] * bi + ci[..., None] * br
    g = lr.shape[0]
    nck = g // SSM_CHUNK_GROUPS
    n_cols = 2 * SSM_CHUNK_STATE
    col = jnp.arange(n_cols)
    col_group = 2 * (col // (2 * LANES)) + (col % LANES) // SSM_STATE
    on_diagonal = (jnp.arange(SSM_CHUNK_GROUPS)[:, None] == col_group[None, :])[None, :, None, :]

    def block_diagonal(w_re, w_im):
        pair = jnp.concatenate([w_re, w_re, w_im, w_im], axis=-1)
        row = jnp.tile(pair, (1, 1, n_cols // pair.shape[-1]))
        row = row.reshape(nck, SSM_CHUNK_GROUPS, SSM_GROUP, n_cols)
        return jnp.where(on_diagonal, row, 0.0).reshape(nck, SSM_CHUNK_IN, n_cols).astype(BF16)

    wb = block_diagonal(jnp.swapaxes(bbar_r, 1, 2), jnp.swapaxes(bbar_i, 1, 2))
    cbt = block_diagonal(c_re.astype(F32), -c_im.astype(F32))
    sr, si = ar, ai
    for _ in range(int(math.log2(s_len))):
        sr, si = sr * sr - si * si, 2.0 * sr * si
    rows = [v.reshape(nck, 1, SSM_CHUNK_STATE) for v in (ar, ai, sr, si)]
    rows += [jnp.zeros_like(rows[0])] * (SUBLANES - len(rows))
    return wb, cbt, jnp.concatenate(rows, axis=1)


def _s5_post_kernel(y_ref, u_ref, z_ref, x_ref, d_ref, wg_ref, bg_ref, wo_ref, lg_ref, lb_ref,
                    xo_ref, xb_ref, *, alpha):
    unblock = lambda ref: jnp.concatenate([ref[j] for j in range(ref.shape[0])], axis=1)
    y = unblock(y_ref) + d_ref[...] * unblock(u_ref)
    g = jax.nn.gelu(y)
    gl = jnp.dot(g.astype(BF16), wg_ref[...], preferred_element_type=F32) + bg_ref[...]
    y2 = g * _sigmoid(gl)
    z = unblock(z_ref)
    a = (y2 * (z * _sigmoid(z))).astype(BF16)
    o = jnp.dot(a, wo_ref[...], preferred_element_type=F32)
    xn = _layer_norm(alpha * x_ref[...] + o, lg_ref[...], lb_ref[...])
    xo_ref[...] = xn
    xb_ref[...] = xn.astype(BF16)


def _s5_post(yssm, uz, x, d, w_glu, b_glu, w_out, ln_g, ln_b, alpha, tm=256):
    nlb, m, _ = yssm.shape
    e = nlb * LANES
    dm = x.shape[1]
    tm = min(tm, m)
    row = lambda i: (i, 0)
    fixed = lambda i: (0, 0)
    return pl.pallas_call(
        functools.partial(_s5_post_kernel, alpha=alpha),
        grid=(m // tm,),
        in_specs=[pl.BlockSpec((nlb, tm, LANES), lambda i: (0, i, 0)),
                  pl.BlockSpec((nlb, tm, LANES), lambda i: (0, i, 0)),
                  pl.BlockSpec((nlb, tm, LANES), lambda i: (1, i, 0)),
                  pl.BlockSpec((tm, dm), row),
                  pl.BlockSpec((1, e), fixed),
                  pl.BlockSpec((e, e), fixed),
                  pl.BlockSpec((1, e), fixed),
                  pl.BlockSpec((e, dm), fixed),
                  pl.BlockSpec((1, dm), fixed),
                  pl.BlockSpec((1, dm), fixed)],
        out_specs=[pl.BlockSpec((tm, dm), row), pl.BlockSpec((tm, dm), row)],
        out_shape=[jax.ShapeDtypeStruct((m, dm), F32), jax.ShapeDtypeStruct((m, dm), BF16)],
        compiler_params=_params("parallel"),
        name="s5_post",
    )(yssm, uz, uz, x, d, w_glu, b_glu, w_out, ln_g, ln_b)


def _compress_kernel(t_ref, pos_ref, w1_ref, w2_ref, o_ref, q_ref, *, nc):
    half = CMP_BLOCK // 2
    acc_p = jnp.zeros((nc, HEAD_DIM), F32)
    acc_q = jnp.zeros((nc, HEAD_DIM), F32)
    for l in range(half):
        t = t_ref[pl.ds(l, nc, stride=CMP_STRIDE), :]
        lo = (t + pos_ref[l:l + 1, :]).astype(BF16)
        hi = (t + pos_ref[half + l:half + l + 1, :]).astype(BF16)
        acc_p += jnp.dot(lo, w1_ref[l * HEAD_DIM:(l + 1) * HEAD_DIM, :], preferred_element_type=F32)
        acc_q += jnp.dot(hi, w1_ref[(half + l) * HEAD_DIM:(half + l + 1) * HEAD_DIM, :],
                         preferred_element_type=F32)
    q_ref[0:nc, :] = acc_q
    q_ref[nc:nc + SUBLANES, :] = jnp.zeros((SUBLANES, HEAD_DIM), F32)
    pre = acc_p + q_ref[pl.ds(1, nc), :]
    mid = jax.nn.gelu(pre).astype(BF16)
    o_ref[...] = jnp.dot(mid, w2_ref[...], preferred_element_type=F32).astype(o_ref.dtype)


def _compress(kvc, pos, w1, w2, bsz, seq):
    nc = seq // CMP_STRIDE
    g = N_KV_HEADS
    return pl.pallas_call(
        functools.partial(_compress_kernel, nc=nc),
        grid=(bsz, 2, g),
        in_specs=[pl.BlockSpec((seq, HEAD_DIM), lambda b, s, h: (b, s * g + h)),
                  pl.BlockSpec((None, CMP_BLOCK, HEAD_DIM), lambda b, s, h: (s, 0, 0)),
                  pl.BlockSpec((None, CMP_BLOCK * HEAD_DIM, HEAD_DIM), lambda b, s, h: (s, 0, 0)),
                  pl.BlockSpec((None, HEAD_DIM, HEAD_DIM), lambda b, s, h: (s, 0, 0))],
        out_specs=pl.BlockSpec((None, None, None, nc, HEAD_DIM), lambda b, s, h: (b, s, h, 0, 0)),
        out_shape=jax.ShapeDtypeStruct((bsz, 2, g, nc, HEAD_DIM), BF16),
        scratch_shapes=[pltpu.VMEM((nc + SUBLANES, HEAD_DIM), F32)],
        compiler_params=_params("parallel", "parallel", "parallel"),
        name="kv_compress",
    )(kvc, pos, w1, w2)


def _cmp_select_kernel(q_ref, k_ref, v_ref, o_ref, sel_ref, *, tq, nc, ns, n_sel):
    hp = HEADS_PER_KV
    step = pl.program_id(2)
    q0 = step * tq
    q4 = jnp.concatenate([q_ref[:, h * HEAD_DIM:(h + 1) * HEAD_DIM] for h in range(hp)], axis=0)
    t = q0 + lax.broadcasted_iota(jnp.int32, (tq, nc), 0)
    n = lax.broadcasted_iota(jnp.int32, (tq, nc), 1)
    valid = (n * CMP_STRIDE + (CMP_BLOCK - 1) <= t)[None]
    s = lax.dot_general(q4, k_ref[...], (((1,), (1,)), ((), ())), preferred_element_type=F32)
    s = jnp.where(valid, s.reshape(hp, tq, nc), MASK_VALUE)
    m = jnp.max(s, axis=-1, keepdims=True)
    e = jnp.where(valid, jnp.exp(s - m), 0.0)
    p = e * (1.0 / jnp.maximum(jnp.sum(e, axis=-1, keepdims=True), 1e-30))
    o = jnp.dot(p.reshape(hp * tq, nc).astype(BF16), v_ref[...], preferred_element_type=F32)
    for h in range(hp):
        o_ref[:, h * HEAD_DIM:(h + 1) * HEAD_DIM] = o[h * tq:(h + 1) * tq, :].astype(o_ref.dtype)
    psum = p[0]
    for h in range(1, hp):
        psum = psum + p[h]

    nn = lax.broadcasted_iota(jnp.int32, (nc, LANES), 0)
    jj = lax.broadcasted_iota(jnp.int32, (nc, LANES), 1)
    overlap = ((nn * CMP_STRIDE < (jj + 1) * SEL_BLOCK)
               & (nn * CMP_STRIDE + (CMP_BLOCK - 1) >= jj * SEL_BLOCK)
               & (jj < ns)).astype(BF16)
    p_hi = psum.astype(BF16)
    p_lo = (psum - p_hi.astype(F32)).astype(BF16)
    imp = (jnp.dot(p_hi, overlap, preferred_element_type=F32)
           + jnp.dot(p_lo, overlap, preferred_element_type=F32))

    tt = q0 + lax.broadcasted_iota(jnp.int32, (tq, LANES), 0)
    j = lax.broadcasted_iota(jnp.int32, (tq, LANES), 1)
    cur = tt // SEL_BLOCK
    forced = (j == 0) | (j == cur) | (j == cur - 1)
    imp = jnp.where(j > cur, MASK_VALUE, jnp.where(forced, FORCE_SCORE, imp))
    imp = jnp.where(j < ns, imp, PAD_SCORE)

    x = imp.T
    sub = lax.broadcasted_iota(jnp.int32, (SUBLANES, tq), 0)

    def rank_and_store(live):
        nblk = live // SUBLANES
        if live <= n_sel:
            picked = [jnp.ones((SUBLANES, tq), F32)] * nblk
        else:
            xs = [x[b * SUBLANES:(b + 1) * SUBLANES, :] for b in range(nblk)]
            cnts = [jnp.zeros((SUBLANES, tq), jnp.int32) for _ in range(nblk)]
            for i in range(live):
                xi = x[i:i + 1, :]
                for b in range(nblk):
                    if b * SUBLANES > i:
                        beats = xi >= xs[b]
                    elif (b + 1) * SUBLANES - 1 <= i:
                        beats = xi > xs[b]
                    else:
                        beats = (xi > xs[b]) | ((xi >= xs[b]) & (sub > i - b * SUBLANES))
                    cnts[b] = cnts[b] + beats.astype(jnp.int32)
            picked = [(c < n_sel).astype(F32) for c in cnts]
        sel_t = jnp.concatenate(picked + [jnp.zeros((LANES - live, tq), F32)] * (live < LANES), axis=0)
        sel_ref[...] = sel_t.T.astype(sel_ref.dtype)

    for c in range(ns * SEL_BLOCK // tq):
        pl.when(step == c)(functools.partial(rank_and_store, (c + 1) * tq // SEL_BLOCK))


def _cmp_select(q, cmp_kv, bsz, seq):
    tq = min(CMP_TQ, seq)
    nq = seq // tq
    nc = seq // CMP_STRIDE
    ns = seq // SEL_BLOCK
    assert ns <= LANES and tq % LANES == 0 and tq % (SUBLANES * SEL_BLOCK) == 0
    g = N_KV_HEADS
    return pl.pallas_call(
        functools.partial(_cmp_select_kernel, tq=tq, nc=nc, ns=ns, n_sel=min(N_SELECT, ns)),
        grid=(bsz, g, nq),
        in_specs=[pl.BlockSpec((tq, GROUP_WIDTH), lambda b, h, i: (b * nq + i, h)),
                  pl.BlockSpec((None, None, None, nc, HEAD_DIM), lambda b, h, i: (b, 0, h, 0, 0)),
                  pl.BlockSpec((None, None, None, nc, HEAD_DIM), lambda b, h, i: (b, 1, h, 0, 0))],
        out_specs=[pl.BlockSpec((tq, GROUP_WIDTH), lambda b, h, i: (b * nq + i, h)),
                   pl.BlockSpec((None, None, tq, LANES), lambda b, h, i: (b, h, i, 0))],
        out_shape=[jax.ShapeDtypeStruct((bsz * seq, N_HEADS * HEAD_DIM), BF16),
                   jax.ShapeDtypeStruct((bsz, g, seq, LANES), BF16)],
        compiler_params=_params("parallel", "parallel", "parallel"),
        name="nsa_cmp_select",
    )(q, cmp_kv, cmp_kv)


def _lane_chunks(v):
    return [v[:, c * LANES:(c + 1) * LANES] for c in range(v.shape[1] // LANES)]


def _softmax_pv(chunks, v):
    mx = chunks[0]
    for sc in chunks[1:]:
        mx = jnp.maximum(mx, sc)
    m = jnp.broadcast_to(jnp.max(mx, axis=-1, keepdims=True), mx.shape)
    es = [jnp.exp2(sc - m) for sc in chunks]
    ls = es[0]
    for e in es[1:]:
        ls = ls + e
    e = jnp.concatenate([e.astype(BF16) for e in es], axis=1)
    return jnp.dot(e, v, preferred_element_type=F32) / jnp.sum(ls, axis=-1, keepdims=True)


def _slc_win_kernel(q_ref, sel_ref, oh_ref, ks_ref, vs_ref, kw_ref, vw_ref, os_ref, ow_ref, s_ref, *,
                    tq, tk, seq):
    hp = HEADS_PER_KV
    rows = hp * tq
    q0 = pl.program_id(2) * tq
    q4 = jnp.concatenate([q_ref[:, h * HEAD_DIM:(h + 1) * HEAD_DIM] for h in range(hp)], axis=0)
    neg = ((sel_ref[...].astype(F32) - 1.0) * (-MASK_VALUE)).astype(BF16)
    q_aug = jnp.concatenate([q4, jnp.concatenate([neg] * hp, axis=0)], axis=1)
    nt_dims = (((1,), (1,)), ((), ()))
    n_tiles = (q0 + tq - 1) // tk + 1
    last = n_tiles - 1

    def masked_scores(kt):
        k0 = pl.multiple_of(kt * tk, tk)
        k_aug = jnp.concatenate([ks_ref[pl.ds(k0, tk), :], oh_ref[pl.ds(k0, tk), :]], axis=1)
        s = lax.dot_general(q_aug, k_aug, nt_dims, preferred_element_type=F32)
        return s * LOG2_E

    def lane_max(s, mx):
        for sc in _lane_chunks(s):
            mx = jnp.maximum(mx, sc)
        return mx

    def scores(kt, mx):
        s = masked_scores(kt)
        s_ref[kt] = s
        return lane_max(s, mx)

    mx = lax.fori_loop(0, last, scores, jnp.full((rows, LANES), MASK_VALUE, F32))
    t = q0 + lax.broadcasted_iota(jnp.int32, (tq, tk), 0)
    kcol = last * tk + lax.broadcasted_iota(jnp.int32, (tq, tk), 1)
    s = jnp.where((kcol <= t)[None], masked_scores(last).reshape(hp, tq, tk), MASK_VALUE).reshape(rows, tk)
    s_ref[last] = s
    mx = lane_max(s, mx)
    m = jnp.broadcast_to(jnp.max(mx, axis=-1, keepdims=True), (rows, LANES))

    def weighted(kt, carry):
        ls, acc = carry
        k0 = pl.multiple_of(kt * tk, tk)
        es = [jnp.exp2(sc - m) for sc in _lane_chunks(s_ref[kt])]
        for e in es:
            ls = ls + e
        e = jnp.concatenate([e.astype(BF16) for e in es], axis=1)
        return ls, acc + jnp.dot(e, vs_ref[pl.ds(k0, tk), :], preferred_element_type=F32)

    ls, acc = lax.fori_loop(0, n_tiles, weighted,
                            (jnp.zeros((rows, LANES), F32), jnp.zeros((rows, HEAD_DIM), F32)))
    o = acc / jnp.sum(ls, axis=-1, keepdims=True)
    for h in range(hp):
        os_ref[:, h * HEAD_DIM:(h + 1) * HEAD_DIM] = o[h * tq:(h + 1) * tq, :].astype(os_ref.dtype)

    tw = min(WIN_TQ, tq)
    span = min(WINDOW + tw, seq)
    for j in range(tq // tw):
        qw0 = q0 + j * tw
        start = pl.multiple_of(jnp.clip(qw0 + tw - span, 0, seq - span), tw)
        kpos = start + lax.broadcasted_iota(jnp.int32, (tw, span), 1)
        tpos = qw0 + lax.broadcasted_iota(jnp.int32, (tw, span), 0)
        wbias = jnp.where((kpos <= tpos) & (kpos > tpos - WINDOW), 0.0, MASK_VALUE)
        qw = jnp.concatenate([q_ref[j * tw:(j + 1) * tw, h * HEAD_DIM:(h + 1) * HEAD_DIM] for h in range(hp)],
                             axis=0)
        sw = lax.dot_general(qw, kw_ref[pl.ds(start, span), :], nt_dims, preferred_element_type=F32) * LOG2_E
        sw = (sw.reshape(hp, tw, span) + wbias[None]).reshape(hp * tw, span)
        ow = _softmax_pv(_lane_chunks(sw), vw_ref[pl.ds(start, span), :])
        for h in range(hp):
            ow_ref[j * tw:(j + 1) * tw, h * HEAD_DIM:(h + 1) * HEAD_DIM] = ow[h * tw:(h + 1) * tw, :].astype(
                ow_ref.dtype)


def _slc_win(q, sel, kvs, bsz, seq):
    tq = min(ATT_TQ, seq)
    tk = min(ATT_TK, seq)
    nq = seq // tq
    g = N_KV_HEADS
    block_onehot = (jnp.arange(seq)[:, None] // SEL_BLOCK == jnp.arange(LANES)[None, :]).astype(BF16)
    kv_spec = lambda part: pl.BlockSpec((seq, HEAD_DIM), lambda b, h, i: (b, part * g + h))
    out_spec = pl.BlockSpec((tq, GROUP_WIDTH), lambda b, h, i: (b * nq + i, h))
    out_sds = jax.ShapeDtypeStruct((bsz * seq, N_HEADS * HEAD_DIM), BF16)
    return pl.pallas_call(
        functools.partial(_slc_win_kernel, tq=tq, tk=tk, seq=seq),
        grid=(bsz, g, nq),
        in_specs=[pl.BlockSpec((tq, GROUP_WIDTH), lambda b, h, i: (b * nq + i, h)),
                  pl.BlockSpec((None, None, tq, LANES), lambda b, h, i: (b, h, i, 0)),
                  pl.BlockSpec((seq, LANES), lambda b, h, i: (0, 0)),
                  kv_spec(0), kv_spec(1), kv_spec(2), kv_spec(3)],
        out_specs=[out_spec, out_spec],
        out_shape=[out_sds, out_sds],
        scratch_shapes=[pltpu.VMEM((seq // tk, HEADS_PER_KV * tq, tk), F32)],
        compiler_params=pltpu.CompilerParams(dimension_semantics=("parallel", "parallel", "parallel"),
                                             vmem_limit_bytes=ATT_VMEM_LIMIT_BYTES),
        name="nsa_slc_win",
    )(q, sel, block_onehot, kvs, kvs, kvs, kvs)


def _nsa_post_kernel(oc_ref, os_ref, ow_ref, sz_ref, gl_ref, x_ref, wo_ref, lg_ref, lb_ref,
                     xo_ref, xb_ref, *, alpha):
    width = N_HEADS * HEAD_DIM
    gate = _sigmoid(gl_ref[...])
    parts = []
    for h in range(N_HEADS):
        hs = slice(h * HEAD_DIM, (h + 1) * HEAD_DIM)
        acc = None
        for br, o_ref in enumerate((oc_ref, os_ref, ow_ref)):
            gcol = gate[:, br * N_HEADS + h:br * N_HEADS + h + 1]
            sz = sz_ref[:, br * width + h * HEAD_DIM:br * width + (h + 1) * HEAD_DIM]
            term = gcol * o_ref[:, hs].astype(F32) * sz.astype(F32)
            acc = term if acc is None else acc + term
        parts.append(acc.astype(BF16))
    y = jnp.concatenate(parts, axis=1)
    o = jnp.dot(y, wo_ref[...], preferred_element_type=F32)
    xn = _layer_norm(alpha * x_ref[...] + o, lg_ref[...], lb_ref[...])
    xo_ref[...] = xn
    xb_ref[...] = xn.astype(BF16)


def _nsa_post(o_cmp, o_slc, o_win, sz, gate_logits, x, w_out, ln_g, ln_b, alpha, tm=256):
    m, dm = x.shape
    width = N_HEADS * HEAD_DIM
    tm = min(tm, m)
    row = lambda i: (i, 0)
    fixed = lambda i: (0, 0)
    return pl.pallas_call(
        functools.partial(_nsa_post_kernel, alpha=alpha),
        grid=(m // tm,),
        in_specs=[pl.BlockSpec((tm, width), row),
                  pl.BlockSpec((tm, width), row),
                  pl.BlockSpec((tm, width), row),
                  pl.BlockSpec((tm, N_BRANCH * width), row),
                  pl.BlockSpec((tm, LANES), row),
                  pl.BlockSpec((tm, dm), row),
                  pl.BlockSpec((width, dm), fixed, pipeline_mode=pl.Buffered(1)),
                  pl.BlockSpec((1, dm), fixed),
                  pl.BlockSpec((1, dm), fixed)],
        out_specs=[pl.BlockSpec((tm, dm), row), pl.BlockSpec((tm, dm), row)],
        out_shape=[jax.ShapeDtypeStruct((m, dm), F32), jax.ShapeDtypeStruct((m, dm), BF16)],
        compiler_params=_params("parallel"),
        name="nsa_post",
    )(o_cmp, o_slc, o_win, sz, gate_logits, x, w_out, ln_g, ln_b)


def kernel(x, a_w_in, a_lam_re, a_lam_im, a_log_dt, a_b_re, a_b_im, a_c_re, a_c_im, a_d, a_w_glu, a_b_glu, a_w_out, kv_w, cmp_pos_k, cmp_w1_k, cmp_w2_k, cmp_pos_v, cmp_w1_v, cmp_w2_v, b_w_in, b_w_out, ln_g, ln_b):
    bsz, seq, dm = x.shape
    n_a = a_w_in.shape[0]
    n_b = b_w_in.shape[0]
    depth = n_a + n_b
    alpha = (2 * depth) ** 0.25
    width = N_HEADS * HEAD_DIM
    kvw = N_KV_HEADS * HEAD_DIM
    s_len = min(SCAN_T, seq) // SUBLANES

    xf = x.reshape(bsz * seq, dm).astype(F32)
    xb = xf.astype(BF16)

    for i in range(n_a):
        wb, cb, acoef = _s5_constants(a_lam_re[i], a_lam_im[i], a_log_dt[i], a_b_re[i], a_b_im[i],
                                      a_c_re[i], a_c_im[i], s_len)
        e = a_w_glu.shape[1]
        uz = _matmul(xb, a_w_in.astype(F32), F32, layer=i, lane_blocked=True)
        yssm = _s5_scan(uz, wb, cb, acoef, bsz, seq)
        xf, xb = _s5_post(yssm, uz, xf, a_d[i].reshape(1, e).astype(F32), a_w_glu[i].astype(BF16),
                          a_b_glu[i].reshape(1, e).astype(F32), a_w_out[i].astype(BF16),
                          ln_g[i].reshape(1, dm).astype(F32), ln_b[i].reshape(1, dm).astype(F32), alpha)

    kvc = _matmul(xb, kv_w.astype(F32), F32, n=2 * kvw)
    kvs = _matmul(xb, kv_w.astype(F32), BF16, col0=2 * kvw, n=4 * kvw)
    cmp_kv = _compress(kvc,
                       jnp.stack([cmp_pos_k, cmp_pos_v]).astype(F32),
                       jnp.stack([cmp_w1_k, cmp_w1_v]).astype(BF16),
                       jnp.stack([cmp_w2_k, cmp_w2_v]).astype(BF16), bsz, seq)

    b_w_in_t = jnp.swapaxes(b_w_in, 1, 2).astype(F32)
    n_main = (1 + N_BRANCH) * width
    for i in range(n_b):
        layer = n_a + i
        w_gate_t = b_w_in_t[i, n_main:, :]
        w_gate_t = jnp.pad(w_gate_t, ((0, LANES - w_gate_t.shape[0]), (0, 0)))
        q = _matmul(xb, b_w_in_t, BF16, layer=i, transposed=True, n=width, scale=HEAD_DIM ** -0.5)
        sz = _matmul(xb, b_w_in_t, BF16, layer=i, transposed=True, col0=width, n=N_BRANCH * width, silu=True)
        gate_logits = _matmul(xb, w_gate_t, F32, transposed=True)
        o_cmp, sel = _cmp_select(q, cmp_kv, bsz, seq)
        o_slc, o_win = _slc_win(q, sel, kvs, bsz, seq)
        xf, xb = _nsa_post(o_cmp, o_slc, o_win, sz, gate_logits, xf, b_w_out[i].astype(BF16),
                           ln_g[layer].reshape(1, dm).astype(F32), ln_b[layer].reshape(1, dm).astype(F32), alpha)

    return xf.reshape(bsz, seq, dm).astype(x.dtype)
```

```python
import functools
import math

import jax
import jax.numpy as jnp
from jax import lax
from jax.experimental import pallas as pl
from jax.experimental.pallas import tpu as pltpu

F32 = jnp.float32
BF16 = jnp.bfloat16

HEAD_DIM = 128
N_HEADS = 16
N_KV_HEADS = 4
HEADS_PER_KV = N_HEADS // N_KV_HEADS
GROUP_WIDTH = HEADS_PER_KV * HEAD_DIM
N_BRANCH = 3
CMP_BLOCK = 32
CMP_STRIDE = 16
SEL_BLOCK = 64
N_SELECT = 16
WINDOW = 512
SSM_GROUP = 16
SSM_STATE = 64
SSM_CHUNK_GROUPS = 16
SSM_CHUNK_IN = SSM_CHUNK_GROUPS * SSM_GROUP
SSM_CHUNK_STATE = SSM_CHUNK_GROUPS * SSM_STATE
LN_EPS = 1e-5
MASK_VALUE = -1e30
FORCE_SCORE = 1e6
PAD_SCORE = -3e38
LOG2_E = math.log2(math.e)

SUBLANES = 8
LANES = 128
VMEM_LIMIT_BYTES = 48 * 1024 * 1024

SCAN_T = 256
CMP_TQ = 512
ATT_TQ = 512
ATT_TK = 512
WIN_TQ = 128
ATT_VMEM_LIMIT_BYTES = 58 * 1024 * 1024


def _params(*sem):
    return pltpu.CompilerParams(dimension_semantics=sem, vmem_limit_bytes=VMEM_LIMIT_BYTES)


def _sigmoid(v):
    return 1.0 / (1.0 + jnp.exp(-v))


def _layer_norm(v, g, b):
    mu = jnp.mean(v, axis=-1, keepdims=True)
    d = v - mu
    var = jnp.mean(d * d, axis=-1, keepdims=True)
    return d * lax.rsqrt(var + LN_EPS) * g + b


def _mm_kernel(a_ref, w_ref, o_ref, *wb_refs, scale, silu, lane_blocked, transposed):
    if wb_refs:
        wb_ref, = wb_refs

        @pl.when(pl.program_id(1) == 0)
        def _():
            wb_ref[...] = w_ref[...].astype(BF16)

        w_ref = wb_ref
    contract = (((1,), (1 if transposed else 0,)), ((), ()))
    acc = lax.dot_general(a_ref[...], w_ref[...], contract, preferred_element_type=F32)
    if scale is not None:
        acc = acc * scale
    if silu:
        acc = acc * _sigmoid(acc)
    if lane_blocked:
        for j in range(o_ref.shape[0]):
            o_ref[j] = acc[:, j * LANES:(j + 1) * LANES].astype(o_ref.dtype)
    else:
        o_ref[...] = acc.astype(o_ref.dtype)


def _matmul(a, w, out_dtype, *, n=None, col0=0, layer=None, transposed=False, tm=512, tn=1024, scale=None,
            silu=False, lane_blocked=False):
    m, k = a.shape
    n = w.shape[-2 if transposed else -1] if n is None else n
    tm = min(tm, m)
    tn = min(tn, n)
    assert m % tm == 0 and n % tn == 0 and col0 % tn == 0
    jb = col0 // tn
    w_block = (tn, k) if transposed else (k, tn)
    w_index = (lambda j: (j + jb, 0)) if transposed else (lambda j: (0, j + jb))
    if layer is None:
        w_spec = pl.BlockSpec(w_block, lambda j, i: w_index(j))
    else:
        w_spec = pl.BlockSpec((None,) + w_block, lambda j, i: (layer,) + w_index(j))
    if lane_blocked:
        out_spec = pl.BlockSpec((tn // LANES, tm, LANES), lambda j, i: (j, i, 0))
        out_shape = jax.ShapeDtypeStruct((n // LANES, m, LANES), out_dtype)
    else:
        out_spec = pl.BlockSpec((tm, tn), lambda j, i: (i, j))
        out_shape = jax.ShapeDtypeStruct((m, n), out_dtype)
    return pl.pallas_call(
        functools.partial(_mm_kernel, scale=scale, silu=silu, lane_blocked=lane_blocked, transposed=transposed),
        grid=(n // tn, m // tm),
        in_specs=[pl.BlockSpec((tm, k), lambda j, i: (i, 0)), w_spec],
        out_specs=out_spec,
        out_shape=out_shape,
        scratch_shapes=[] if w.dtype == BF16 else [pltpu.VMEM(w_block, BF16)],
        compiler_params=_params("parallel", "arbitrary"),
        name="matmul",
    )(a, w)


def _s5_scan_kernel(u_ref, wb_ref, cb_ref, a_ref, o_ref, bu_ref, hb_ref, carry_ref, *, t_tok):
    s_len = t_tok // SUBLANES
    p = SSM_CHUNK_STATE

    @pl.when(pl.program_id(2) == 0)
    def _():
        carry_ref[...] = jnp.zeros_like(carry_ref)
        hb_ref[...] = jnp.zeros_like(hb_ref)

    n_lb = SSM_CHUNK_IN // LANES
    rows = [jnp.concatenate([u_ref[c, pl.ds(k, SUBLANES, stride=s_len), :] for c in range(n_lb)], axis=1)
            for k in range(s_len)]
    up = jnp.concatenate(rows, axis=0).astype(BF16)
    bu_ref[...] = jnp.dot(up, wb_ref[...], preferred_element_type=F32)

    y = lax.dot_general(hb_ref[...].astype(BF16), cb_ref[...], (((1,), (1,)), ((), ())),
                        preferred_element_type=F32)
    for k in range(s_len):
        for c in range(n_lb):
            o_ref[c, pl.ds(k, SUBLANES, stride=s_len), :] = y[k * SUBLANES:(k + 1) * SUBLANES,
                                                               c * LANES:(c + 1) * LANES]

    sub = lax.broadcasted_iota(jnp.int32, (SUBLANES, LANES), 0)
    for blk in range(p // LANES):
        re = slice(2 * blk * LANES, (2 * blk + 1) * LANES)
        im = slice((2 * blk + 1) * LANES, (2 * blk + 2) * LANES)
        st = slice(blk * LANES, (blk + 1) * LANES)
        ar = jnp.broadcast_to(a_ref[0:1, st], (SUBLANES, LANES))
        ai = jnp.broadcast_to(a_ref[1:2, st], (SUBLANES, LANES))
        sr = jnp.broadcast_to(a_ref[2:3, st], (SUBLANES, LANES))
        si = jnp.broadcast_to(a_ref[3:4, st], (SUBLANES, LANES))

        def step(k, hr, hi):
            br = bu_ref[k * SUBLANES:(k + 1) * SUBLANES, re]
            bi = bu_ref[k * SUBLANES:(k + 1) * SUBLANES, im]
            return ar * hr - ai * hi + br, ar * hi + ai * hr + bi

        er = jnp.zeros((SUBLANES, LANES), F32)
        ei = jnp.zeros((SUBLANES, LANES), F32)
        for k in range(s_len):
            er, ei = step(k, er, ei)

        hinr = jnp.where(sub == 0, carry_ref[:, re], 0.0)
        hini = jnp.where(sub == 0, carry_ref[:, im], 0.0)
        for j in range(SUBLANES - 1):
            nr = sr * hinr - si * hini + er
            ni = sr * hini + si * hinr + ei
            hinr = jnp.where(sub == j + 1, pltpu.roll(nr, 1, 0), hinr)
            hini = jnp.where(sub == j + 1, pltpu.roll(ni, 1, 0), hini)
        nr = sr * hinr - si * hini + er
        ni = sr * hini + si * hinr + ei
        carry_ref[:, re] = jnp.broadcast_to(nr[SUBLANES - 1:SUBLANES, :], (SUBLANES, LANES))
        carry_ref[:, im] = jnp.broadcast_to(ni[SUBLANES - 1:SUBLANES, :], (SUBLANES, LANES))

        hr, hi = hinr, hini
        for k in range(s_len):
            hr, hi = step(k, hr, hi)
            hb_ref[k * SUBLANES:(k + 1) * SUBLANES, re] = hr
            hb_ref[k * SUBLANES:(k + 1) * SUBLANES, im] = hi


def _s5_scan(uz, wb, cb, acoef, bsz, seq):
    n_chunks = wb.shape[0]
    e = n_chunks * SSM_CHUNK_IN
    n_lb = SSM_CHUNK_IN // LANES
    t_tok = min(SCAN_T, seq)
    nt = seq // t_tok
    assert seq % t_tok == 0 and t_tok % (SUBLANES * SUBLANES) == 0
    return pl.pallas_call(
        functools.partial(_s5_scan_kernel, t_tok=t_tok),
        grid=(bsz, n_chunks, nt + 1),
        in_specs=[
            pl.BlockSpec((n_lb, t_tok, LANES), lambda b, c, t: (c, b * nt + jnp.minimum(t, nt - 1), 0)),
            pl.BlockSpec((None, SSM_CHUNK_IN, 2 * SSM_CHUNK_STATE), lambda b, c, t: (c, 0, 0)),
            pl.BlockSpec((None, SSM_CHUNK_IN, 2 * SSM_CHUNK_STATE), lambda b, c, t: (c, 0, 0)),
            pl.BlockSpec((None, SUBLANES, SSM_CHUNK_STATE), lambda b, c, t: (c, 0, 0)),
        ],
        out_specs=pl.BlockSpec((n_lb, t_tok, LANES), lambda b, c, t: (c, b * nt + jnp.maximum(t - 1, 0), 0)),
        out_shape=jax.ShapeDtypeStruct((e // LANES, bsz * seq, LANES), F32),
        scratch_shapes=[pltpu.VMEM((t_tok, 2 * SSM_CHUNK_STATE), F32),
                        pltpu.VMEM((t_tok, 2 * SSM_CHUNK_STATE), F32),
                        pltpu.VMEM((SUBLANES, 2 * SSM_CHUNK_STATE), F32)],
        compiler_params=_params("parallel", "parallel", "arbitrary"),
        name="s5_scan",
    )(uz, wb, cb, acoef)


def _s5_constants(lam_re, lam_im, log_dt, b_re, b_im, c_re, c_im, s_len):
    lr = lam_re.astype(F32)
    li = lam_im.astype(F32)
    dt = jnp.exp(log_dt.astype(F32))[:, None]
    mag = jnp.exp(lr * dt)
    ar = mag * jnp.cos(li * dt)
    ai = mag * jnp.sin(li * dt)
    inv_abs2 = 1.0 / (lr * lr + li * li)
    cr = ((ar - 1.0) * lr + ai * li) * inv_abs2
    ci = (ai * lr - (ar - 1.0) * li) * inv_abs2
    br = b_re.astype(F32)
    bi = b_im.astype(F32)
    bbar_r = cr[..., None] * br - ci[..., None] * bi
    bbar_i = cr[..., None] * bi + ci[..., None] * br
    g = lr.shape[0]
    nck = g // SSM_CHUNK_GROUPS
    n_cols = 2 * SSM_CHUNK_STATE
    col = jnp.arange(n_cols)
    col_group = 2 * (col // (2 * LANES)) + (col % LANES) // SSM_STATE
    on_diagonal = (jnp.arange(SSM_CHUNK_GROUPS)[:, None] == col_group[None, :])[None, :, None, :]

    def block_diagonal(w_re, w_im):
        pair = jnp.concatenate([w_re, w_re, w_im, w_im], axis=-1)
        row = jnp.tile(pair, (1, 1, n_cols // pair.shape[-1]))
        row = row.reshape(nck, SSM_CHUNK_GROUPS, SSM_GROUP, n_cols)
        return jnp.where(on_diagonal, row, 0.0).reshape(nck, SSM_CHUNK_IN, n_cols).astype(BF16)

    wb = block_diagonal(jnp.swapaxes(bbar_r, 1, 2), jnp.swapaxes(bbar_i, 1, 2))
    cbt = block_diagonal(c_re.astype(F32), -c_im.astype(F32))
    sr, si = ar, ai
    for _ in range(int(math.log2(s_len))):
        sr, si = sr * sr - si * si, 2.0 * sr * si
    rows = [v.reshape(nck, 1, SSM_CHUNK_STATE) for v in (ar, ai, sr, si)]
    rows += [jnp.zeros_like(rows[0])] * (SUBLANES - len(rows))
    return wb, cbt, jnp.concatenate(rows, axis=1)


def _lagged_out_proj(a_ref, x_ref, wo_ref, lg_ref, lb_ref, xo_ref, xb_ref, alpha):
    @pl.when(pl.program_id(0) == 0)
    def _():
        a_ref[...] = jnp.zeros_like(a_ref)

    o = jnp.dot(a_ref[...], wo_ref[...], preferred_element_type=F32)
    xn = _layer_norm(alpha * x_ref[...] + o, lg_ref[...], lb_ref[...])
    xo_ref[...] = xn
    xb_ref[...] = xn.astype(BF16)


def _s5_post_kernel(y_ref, u_ref, z_ref, x_ref, d_ref, wg_ref, bg_ref, wo_ref, lg_ref, lb_ref,
                    xo_ref, xb_ref, *, alpha):
    unblock = lambda ref: jnp.concatenate([ref[j] for j in range(ref.shape[0])], axis=1)
    y = unblock(y_ref) + d_ref[...] * unblock(u_ref)
    g = jax.nn.gelu(y)
    gl = jnp.dot(g.astype(BF16), wg_ref[...], preferred_element_type=F32) + bg_ref[...]
    y2 = g * _sigmoid(gl)
    z = unblock(z_ref)
    a = (y2 * (z * _sigmoid(z))).astype(BF16)
    o = jnp.dot(a, wo_ref[...], preferred_element_type=F32)
    xn = _layer_norm(alpha * x_ref[...] + o, lg_ref[...], lb_ref[...])
    xo_ref[...] = xn
    xb_ref[...] = xn.astype(BF16)


def _s5_post(yssm, uz, x, d, w_glu, b_glu, w_out, ln_g, ln_b, alpha, tm=256):
    nlb, m, _ = yssm.shape
    e = nlb * LANES
    dm = x.shape[1]
    tm = min(tm, m)
    row = lambda i: (i, 0)
    fixed = lambda i: (0, 0)
    return pl.pallas_call(
        functools.partial(_s5_post_kernel, alpha=alpha),
        grid=(m // tm,),
        in_specs=[pl.BlockSpec((nlb, tm, LANES), lambda i: (0, i, 0)),
                  pl.BlockSpec((nlb, tm, LANES), lambda i: (0, i, 0)),
                  pl.BlockSpec((nlb, tm, LANES), lambda i: (1, i, 0)),
                  pl.BlockSpec((tm, dm), row),
                  pl.BlockSpec((1, e), fixed),
                  pl.BlockSpec((e, e), fixed),
                  pl.BlockSpec((1, e), fixed),
                  pl.BlockSpec((e, dm), fixed),
                  pl.BlockSpec((1, dm), fixed),
                  pl.BlockSpec((1, dm), fixed)],
        out_specs=[pl.BlockSpec((tm, dm), row), pl.BlockSpec((tm, dm), row)],
        out_shape=[jax.ShapeDtypeStruct((m, dm), F32), jax.ShapeDtypeStruct((m, dm), BF16)],
        compiler_params=_params("parallel"),
        name="s5_post",
    )(yssm, uz, uz, x, d, w_glu, b_glu, w_out, ln_g, ln_b)


def _compress_kernel(t_ref, pos_ref, w1_ref, w2_ref, o_ref, q_ref, *, nc):
    half = CMP_BLOCK // 2
    acc_p = jnp.zeros((nc, HEAD_DIM), F32)
    acc_q = jnp.zeros((nc, HEAD_DIM), F32)
    for l in range(half):
        t = t_ref[pl.ds(l, nc, stride=CMP_STRIDE), :]
        lo = (t + pos_ref[l:l + 1, :]).astype(BF16)
        hi = (t + pos_ref[half + l:half + l + 1, :]).astype(BF16)
        acc_p += jnp.dot(lo, w1_ref[l * HEAD_DIM:(l + 1) * HEAD_DIM, :], preferred_element_type=F32)
        acc_q += jnp.dot(hi, w1_ref[(half + l) * HEAD_DIM:(half + l + 1) * HEAD_DIM, :],
                         preferred_element_type=F32)
    q_ref[0:nc, :] = acc_q
    q_ref[nc:nc + SUBLANES, :] = jnp.zeros((SUBLANES, HEAD_DIM), F32)
    pre = acc_p + q_ref[pl.ds(1, nc), :]
    mid = jax.nn.gelu(pre).astype(BF16)
    o_ref[...] = jnp.dot(mid, w2_ref[...], preferred_element_type=F32).astype(o_ref.dtype)


def _compress(kvc, pos, w1, w2, bsz, seq):
    nc = seq // CMP_STRIDE
    g = N_KV_HEADS
    return pl.pallas_call(
        functools.partial(_compress_kernel, nc=nc),
        grid=(bsz, 2, g),
        in_specs=[pl.BlockSpec((seq, HEAD_DIM), lambda b, s, h: (b, s * g + h)),
                  pl.BlockSpec((None, CMP_BLOCK, HEAD_DIM), lambda b, s, h: (s, 0, 0)),
                  pl.BlockSpec((None, CMP_BLOCK * HEAD_DIM, HEAD_DIM), lambda b, s, h: (s, 0, 0)),
                  pl.BlockSpec((None, HEAD_DIM, HEAD_DIM), lambda b, s, h: (s, 0, 0))],
        out_specs=pl.BlockSpec((None, None, None, nc, HEAD_DIM), lambda b, s, h: (b, s, h, 0, 0)),
        out_shape=jax.ShapeDtypeStruct((bsz, 2, g, nc, HEAD_DIM), BF16),
        scratch_shapes=[pltpu.VMEM((nc + SUBLANES, HEAD_DIM), F32)],
        compiler_params=_params("parallel", "parallel", "parallel"),
        name="kv_compress",
    )(kvc, pos, w1, w2)


def _cmp_select_kernel(q_ref, k_ref, v_ref, o_ref, sel_ref, *, tq, nc, ns, n_sel):
    hp = HEADS_PER_KV
    step = pl.program_id(2)
    q0 = step * tq
    q4 = jnp.concatenate([q_ref[:, h * HEAD_DIM:(h + 1) * HEAD_DIM] for h in range(hp)], axis=0)
    t = q0 + lax.broadcasted_iota(jnp.int32, (tq, nc), 0)
    n = lax.broadcasted_iota(jnp.int32, (tq, nc), 1)
    valid = (n * CMP_STRIDE + (CMP_BLOCK - 1) <= t)[None]
    s = lax.dot_general(q4, k_ref[...], (((1,), (1,)), ((), ())), preferred_element_type=F32)
    s = jnp.where(valid, s.reshape(hp, tq, nc), MASK_VALUE)
    m = jnp.max(s, axis=-1, keepdims=True)
    e = jnp.where(valid, jnp.exp(s - m), 0.0)
    p = e * (1.0 / jnp.maximum(jnp.sum(e, axis=-1, keepdims=True), 1e-30))
    o = jnp.dot(p.reshape(hp * tq, nc).astype(BF16), v_ref[...], preferred_element_type=F32)
    for h in range(hp):
        o_ref[:, h * HEAD_DIM:(h + 1) * HEAD_DIM] = o[h * tq:(h + 1) * tq, :].astype(o_ref.dtype)
    psum = p[0]
    for h in range(1, hp):
        psum = psum + p[h]

    nn = lax.broadcasted_iota(jnp.int32, (nc, LANES), 0)
    jj = lax.broadcasted_iota(jnp.int32, (nc, LANES), 1)
    overlap = ((nn * CMP_STRIDE < (jj + 1) * SEL_BLOCK)
               & (nn * CMP_STRIDE + (CMP_BLOCK - 1) >= jj * SEL_BLOCK)
               & (jj < ns)).astype(BF16)
    p_hi = psum.astype(BF16)
    p_lo = (psum - p_hi.astype(F32)).astype(BF16)
    imp = (jnp.dot(p_hi, overlap, preferred_element_type=F32)
           + jnp.dot(p_lo, overlap, preferred_element_type=F32))

    tt = q0 + lax.broadcasted_iota(jnp.int32, (tq, LANES), 0)
    j = lax.broadcasted_iota(jnp.int32, (tq, LANES), 1)
    cur = tt // SEL_BLOCK
    forced = (j == 0) | (j == cur) | (j == cur - 1)
    imp = jnp.where(j > cur, MASK_VALUE, jnp.where(forced, FORCE_SCORE, imp))
    imp = jnp.where(j < ns, imp, PAD_SCORE)

    x = imp.T
    sub = lax.broadcasted_iota(jnp.int32, (SUBLANES, tq), 0)

    def rank_and_store(live):
        nblk = live // SUBLANES
        if live <= n_sel:
            picked = [jnp.ones((SUBLANES, tq), F32)] * nblk
        else:
            xs = [x[b * SUBLANES:(b + 1) * SUBLANES, :] for b in range(nblk)]
            cnts = [jnp.zeros((SUBLANES, tq), jnp.int32) for _ in range(nblk)]
            for i in range(live):
                xi = x[i:i + 1, :]
                for b in range(nblk):
                    if b * SUBLANES > i:
                        beats = xi >= xs[b]
                    elif (b + 1) * SUBLANES - 1 <= i:
                        beats = xi > xs[b]
                    else:
                        beats = (xi > xs[b]) | ((xi >= xs[b]) & (sub > i - b * SUBLANES))
                    cnts[b] = cnts[b] + beats.astype(jnp.int32)
            picked = [(c < n_sel).astype(F32) for c in cnts]
        sel_t = jnp.concatenate(picked + [jnp.zeros((LANES - live, tq), F32)] * (live < LANES), axis=0)
        sel_ref[...] = sel_t.T.astype(sel_ref.dtype)

    for c in range(ns * SEL_BLOCK // tq):
        pl.when(step == c)(functools.partial(rank_and_store, (c + 1) * tq // SEL_BLOCK))


def _cmp_select(q, cmp_kv, bsz, seq):
    tq = min(CMP_TQ, seq)
    nq = seq // tq
    nc = seq // CMP_STRIDE
    ns = seq // SEL_BLOCK
    assert ns <= LANES and tq % LANES == 0 and tq % (SUBLANES * SEL_BLOCK) == 0
    g = N_KV_HEADS
    return pl.pallas_call(
        functools.partial(_cmp_select_kernel, tq=tq, nc=nc, ns=ns, n_sel=min(N_SELECT, ns)),
        grid=(bsz, g, nq),
        in_specs=[pl.BlockSpec((tq, GROUP_WIDTH), lambda b, h, i: (b * nq + i, h)),
                  pl.BlockSpec((None, None, None, nc, HEAD_DIM), lambda b, h, i: (b, 0, h, 0, 0)),
                  pl.BlockSpec((None, None, None, nc, HEAD_DIM), lambda b, h, i: (b, 1, h, 0, 0))],
        out_specs=[pl.BlockSpec((tq, GROUP_WIDTH), lambda b, h, i: (b * nq + i, h)),
                   pl.BlockSpec((None, None, tq, LANES), lambda b, h, i: (b, h, i, 0))],
        out_shape=[jax.ShapeDtypeStruct((bsz * seq, N_HEADS * HEAD_DIM), BF16),
                   jax.ShapeDtypeStruct((bsz, g, seq, LANES), BF16)],
        compiler_params=_params("parallel", "parallel", "parallel"),
        name="nsa_cmp_select",
    )(q, cmp_kv, cmp_kv)


def _lane_chunks(v):
    return [v[:, c * LANES:(c + 1) * LANES] for c in range(v.shape[1] // LANES)]


def _softmax_pv(chunks, v):
    mx = chunks[0]
    for sc in chunks[1:]:
        mx = jnp.maximum(mx, sc)
    m = jnp.broadcast_to(jnp.max(mx, axis=-1, keepdims=True), mx.shape)
    es = [jnp.exp2(sc - m) for sc in chunks]
    ls = es[0]
    for e in es[1:]:
        ls = ls + e
    e = jnp.concatenate([e.astype(BF16) for e in es], axis=1)
    return jnp.dot(e, v, preferred_element_type=F32) / jnp.sum(ls, axis=-1, keepdims=True)


def _slc_win_kernel(q_ref, sel_ref, oh_ref, ks_ref, vs_ref, kw_ref, vw_ref, os_ref, ow_ref, s_ref, *,
                    tq, tk, seq):
    hp = HEADS_PER_KV
    rows = hp * tq
    q0 = pl.program_id(2) * tq
    q4 = jnp.concatenate([q_ref[:, h * HEAD_DIM:(h + 1) * HEAD_DIM] for h in range(hp)], axis=0)
    neg = ((sel_ref[...].astype(F32) - 1.0) * (-MASK_VALUE)).astype(BF16)
    q_aug = jnp.concatenate([q4, jnp.concatenate([neg] * hp, axis=0)], axis=1)
    nt_dims = (((1,), (1,)), ((), ()))
    n_tiles = (q0 + tq - 1) // tk + 1
    last = n_tiles - 1

    def masked_scores(kt):
        k0 = pl.multiple_of(kt * tk, tk)
        k_aug = jnp.concatenate([ks_ref[pl.ds(k0, tk), :], oh_ref[pl.ds(k0, tk), :]], axis=1)
        s = lax.dot_general(q_aug, k_aug, nt_dims, preferred_element_type=F32)
        return s * LOG2_E

    def lane_max(s, mx):
        for sc in _lane_chunks(s):
            mx = jnp.maximum(mx, sc)
        return mx

    def scores(kt, mx):
        s = masked_scores(kt)
        s_ref[kt] = s
        return lane_max(s, mx)

    mx = lax.fori_loop(0, last, scores, jnp.full((rows, LANES), MASK_VALUE, F32))
    t = q0 + lax.broadcasted_iota(jnp.int32, (tq, tk), 0)
    kcol = last * tk + lax.broadcasted_iota(jnp.int32, (tq, tk), 1)
    s = jnp.where((kcol <= t)[None], masked_scores(last).reshape(hp, tq, tk), MASK_VALUE).reshape(rows, tk)
    s_ref[last] = s
    mx = lane_max(s, mx)
    m = jnp.broadcast_to(jnp.max(mx, axis=-1, keepdims=True), (rows, LANES))

    def weighted(kt, carry):
        ls, acc = carry
        k0 = pl.multiple_of(kt * tk, tk)
        es = [jnp.exp2(sc - m) for sc in _lane_chunks(s_ref[kt])]
        for e in es:
            ls = ls + e
        e = jnp.concatenate([e.astype(BF16) for e in es], axis=1)
        return ls, acc + jnp.dot(e, vs_ref[pl.ds(k0, tk), :], preferred_element_type=F32)

    ls, acc = lax.fori_loop(0, n_tiles, weighted,
                            (jnp.zeros((rows, LANES), F32), jnp.zeros((rows, HEAD_DIM), F32)))
    o = acc / jnp.sum(ls, axis=-1, keepdims=True)
    for h in range(hp):
        os_ref[:, h * HEAD_DIM:(h + 1) * HEAD_DIM] = o[h * tq:(h + 1) * tq, :].astype(os_ref.dtype)

    tw = min(WIN_TQ, tq)
    span = min(WINDOW + tw, seq)
    for j in range(tq // tw):
        qw0 = q0 + j * tw
        start = pl.multiple_of(jnp.clip(qw0 + tw - span, 0, seq - span), tw)
        kpos = start + lax.broadcasted_iota(jnp.int32, (tw, span), 1)
        tpos = qw0 + lax.broadcasted_iota(jnp.int32, (tw, span), 0)
        wbias = jnp.where((kpos <= tpos) & (kpos > tpos - WINDOW), 0.0, MASK_VALUE)
        qw = jnp.concatenate([q_ref[j * tw:(j + 1) * tw, h * HEAD_DIM:(h + 1) * HEAD_DIM] for h in range(hp)],
                             axis=0)
        sw = lax.dot_general(qw, kw_ref[pl.ds(start, span), :], nt_dims, preferred_element_type=F32) * LOG2_E
        sw = (sw.reshape(hp, tw, span) + wbias[None]).reshape(hp * tw, span)
        ow = _softmax_pv(_lane_chunks(sw), vw_ref[pl.ds(start, span), :])
        for h in range(hp):
            ow_ref[j * tw:(j + 1) * tw, h * HEAD_DIM:(h + 1) * HEAD_DIM] = ow[h * tw:(h + 1) * tw, :].astype(
                ow_ref.dtype)


def _slc_win(q, sel, kvs, bsz, seq):
    tq = min(ATT_TQ, seq)
    tk = min(ATT_TK, seq)
    nq = seq // tq
    g = N_KV_HEADS
    block_onehot = (jnp.arange(seq)[:, None] // SEL_BLOCK == jnp.arange(LANES)[None, :]).astype(BF16)
    kv_spec = lambda part: pl.BlockSpec((seq, HEAD_DIM), lambda b, h, i: (b, part * g + h))
    out_spec = pl.BlockSpec((tq, GROUP_WIDTH), lambda b, h, i: (b * nq + i, h))
    out_sds = jax.ShapeDtypeStruct((bsz * seq, N_HEADS * HEAD_DIM), BF16)
    return pl.pallas_call(
        functools.partial(_slc_win_kernel, tq=tq, tk=tk, seq=seq),
        grid=(bsz, g, nq),
        in_specs=[pl.BlockSpec((tq, GROUP_WIDTH), lambda b, h, i: (b * nq + i, h)),
                  pl.BlockSpec((None, None, tq, LANES), lambda b, h, i: (b, h, i, 0)),
                  pl.BlockSpec((seq, LANES), lambda b, h, i: (0, 0)),
                  kv_spec(0), kv_spec(1), kv_spec(2), kv_spec(3)],
        out_specs=[out_spec, out_spec],
        out_shape=[out_sds, out_sds],
        scratch_shapes=[pltpu.VMEM((seq // tk, HEADS_PER_KV * tq, tk), F32)],
        compiler_params=pltpu.CompilerParams(dimension_semantics=("parallel", "parallel", "parallel"),
                                             vmem_limit_bytes=ATT_VMEM_LIMIT_BYTES),
        name="nsa_slc_win",
    )(q, sel, block_onehot, kvs, kvs, kvs, kvs)


def _nsa_post_kernel(oc_ref, os_ref, ow_ref, sz_ref, gl_ref, x_ref, wo_ref, lg_ref, lb_ref,
                     xo_ref, xb_ref, y_ref, *, alpha):
    _lagged_out_proj(y_ref, x_ref, wo_ref, lg_ref, lb_ref, xo_ref, xb_ref, alpha)
    width = N_HEADS * HEAD_DIM
    gate = _sigmoid(gl_ref[...])
    for h in range(N_HEADS):
        hs = slice(h * HEAD_DIM, (h + 1) * HEAD_DIM)
        acc = None
        for br, o_ref in enumerate((oc_ref, os_ref, ow_ref)):
            gcol = gate[:, br * N_HEADS + h:br * N_HEADS + h + 1]
            sz = sz_ref[:, br * width + h * HEAD_DIM:br * width + (h + 1) * HEAD_DIM]
            term = gcol * o_ref[:, hs].astype(F32) * sz.astype(F32)
            acc = term if acc is None else acc + term
        y_ref[:, hs] = acc.astype(BF16)


def _nsa_post(o_cmp, o_slc, o_win, sz, gate_logits, x, w_out, ln_g, ln_b, alpha, tm=256):
    m, dm = x.shape
    width = N_HEADS * HEAD_DIM
    tm = min(tm, m)
    nt = m // tm
    cur = lambda i: (jnp.minimum(i, nt - 1), 0)
    prev = lambda i: (jnp.maximum(i - 1, 0), 0)
    fixed = lambda i: (0, 0)
    return pl.pallas_call(
        functools.partial(_nsa_post_kernel, alpha=alpha),
        grid=(nt + 1,),
        in_specs=[pl.BlockSpec((tm, width), cur),
                  pl.BlockSpec((tm, width), cur),
                  pl.BlockSpec((tm, width), cur),
                  pl.BlockSpec((tm, N_BRANCH * width), cur),
                  pl.BlockSpec((tm, LANES), cur),
                  pl.BlockSpec((tm, dm), prev),
                  pl.BlockSpec((width, dm), fixed, pipeline_mode=pl.Buffered(1)),
                  pl.BlockSpec((1, dm), fixed),
                  pl.BlockSpec((1, dm), fixed)],
        out_specs=[pl.BlockSpec((tm, dm), prev), pl.BlockSpec((tm, dm), prev)],
        out_shape=[jax.ShapeDtypeStruct((m, dm), F32), jax.ShapeDtypeStruct((m, dm), BF16)],
        scratch_shapes=[pltpu.VMEM((tm, width), BF16)],
        compiler_params=_params("arbitrary"),
        name="nsa_post",
    )(o_cmp, o_slc, o_win, sz, gate_logits, x, w_out, ln_g, ln_b)


def kernel(x, a_w_in, a_lam_re, a_lam_im, a_log_dt, a_b_re, a_b_im, a_c_re, a_c_im, a_d, a_w_glu, a_b_glu, a_w_out, kv_w, cmp_pos_k, cmp_w1_k, cmp_w2_k, cmp_pos_v, cmp_w1_v, cmp_w2_v, b_w_in, b_w_out, ln_g, ln_b):
    bsz, seq, dm = x.shape
    n_a = a_w_in.shape[0]
    n_b = b_w_in.shape[0]
    depth = n_a + n_b
    alpha = (2 * depth) ** 0.25
    width = N_HEADS * HEAD_DIM
    kvw = N_KV_HEADS * HEAD_DIM
    s_len = min(SCAN_T, seq) // SUBLANES

    xf = x.reshape(bsz * seq, dm).astype(F32)
    xb = xf.astype(BF16)

    for i in range(n_a):
        wb, cb, acoef = _s5_constants(a_lam_re[i], a_lam_im[i], a_log_dt[i], a_b_re[i], a_b_im[i],
                                      a_c_re[i], a_c_im[i], s_len)
        e = a_w_glu.shape[1]
        uz = _matmul(xb, a_w_in.astype(F32), F32, layer=i, lane_blocked=True)
        yssm = _s5_scan(uz, wb, cb, acoef, bsz, seq)
        xf, xb = _s5_post(yssm, uz, xf, a_d[i].reshape(1, e).astype(F32), a_w_glu[i].astype(BF16),
                          a_b_glu[i].reshape(1, e).astype(F32), a_w_out[i].astype(BF16),
                          ln_g[i].reshape(1, dm).astype(F32), ln_b[i].reshape(1, dm).astype(F32), alpha)

    kvc = _matmul(xb, kv_w.astype(F32), F32, n=2 * kvw)
    kvs = _matmul(xb, kv_w.astype(F32), BF16, col0=2 * kvw, n=4 * kvw)
    cmp_kv = _compress(kvc,
                       jnp.stack([cmp_pos_k, cmp_pos_v]).astype(F32),
                       jnp.stack([cmp_w1_k, cmp_w1_v]).astype(BF16),
                       jnp.stack([cmp_w2_k, cmp_w2_v]).astype(BF16), bsz, seq)

    b_w_in_t = jnp.swapaxes(b_w_in, 1, 2).astype(F32)
    n_main = (1 + N_BRANCH) * width
    for i in range(n_b):
        layer = n_a + i
        w_gate_t = b_w_in_t[i, n_main:, :]
        w_gate_t = jnp.pad(w_gate_t, ((0, LANES - w_gate_t.shape[0]), (0, 0)))
        q = _matmul(xb, b_w_in_t, BF16, layer=i, transposed=True, n=width, scale=HEAD_DIM ** -0.5)
        sz = _matmul(xb, b_w_in_t, BF16, layer=i, transposed=True, col0=width, n=N_BRANCH * width, silu=True)
        gate_logits = _matmul(xb, w_gate_t, F32, transposed=True)
        o_cmp, sel = _cmp_select(q, cmp_kv, bsz, seq)
        o_slc, o_win = _slc_win(q, sel, kvs, bsz, seq)
        xf, xb = _nsa_post(o_cmp, o_slc, o_win, sz, gate_logits, xf, b_w_out[i].astype(BF16),
                           ln_g[layer].reshape(1, dm).astype(F32), ln_b[layer].reshape(1, dm).astype(F32), alpha)

    return xf.reshape(bsz, seq, dm).astype(x.dtype)
```

```python
import functools
import math

import jax
import jax.numpy as jnp
from jax import lax
from jax.experimental import pallas as pl
from jax.experimental.pallas import tpu as pltpu

F32 = jnp.float32
BF16 = jnp.bfloat16

HEAD_DIM = 128
N_HEADS = 16
N_KV_HEADS = 4
HEADS_PER_KV = N_HEADS // N_KV_HEADS
GROUP_WIDTH = HEADS_PER_KV * HEAD_DIM
N_BRANCH = 3
CMP_BLOCK = 32
CMP_STRIDE = 16
SEL_BLOCK = 64
N_SELECT = 16
WINDOW = 512
SSM_GROUP = 16
SSM_STATE = 64
SSM_CHUNK_GROUPS = 16
SSM_CHUNK_IN = SSM_CHUNK_GROUPS * SSM_GROUP
SSM_CHUNK_STATE = SSM_CHUNK_GROUPS * SSM_STATE
LN_EPS = 1e-5
MASK_VALUE = -1e30
FORCE_SCORE = 1e6
PAD_SCORE = -3e38
LOG2_E = math.log2(math.e)

SUBLANES = 8
LANES = 128
VMEM_LIMIT_BYTES = 48 * 1024 * 1024

SCAN_T = 512
CMP_TQ = 512
ATT_TQ = 512
ATT_TK = 512
WIN_TQ = 128
ATT_VMEM_LIMIT_BYTES = 58 * 1024 * 1024


def _params(*sem):
    return pltpu.CompilerParams(dimension_semantics=sem, vmem_limit_bytes=VMEM_LIMIT_BYTES)


def _sigmoid(v):
    return 0.5 * jnp.tanh(0.5 * v) + 0.5


def _layer_norm(v, g, b):
    mu = jnp.mean(v, axis=-1, keepdims=True)
    d = v - mu
    var = jnp.mean(d * d, axis=-1, keepdims=True)
    return d * lax.rsqrt(var + LN_EPS) * g + b


def _mm_kernel(a_ref, w_ref, o_ref, *wb_refs, scale, silu, lane_blocked, transposed):
    if wb_refs:
        wb_ref, = wb_refs

        @pl.when(pl.program_id(1) == 0)
        def _():
            wb_ref[...] = w_ref[...].astype(BF16)

        w_ref = wb_ref
    contract = (((1,), (1 if transposed else 0,)), ((), ()))
    acc = lax.dot_general(a_ref[...].astype(BF16), w_ref[...], contract, preferred_element_type=F32)
    if scale is not None:
        acc = acc * scale
    if silu:
        acc = acc * _sigmoid(acc)
    if lane_blocked:
        for j in range(o_ref.shape[0]):
            o_ref[j] = acc[:, j * LANES:(j + 1) * LANES].astype(o_ref.dtype)
    else:
        o_ref[...] = acc.astype(o_ref.dtype)


def _matmul(a, w, out_dtype, *, n=None, col0=0, layer=None, transposed=False, tm=512, tn=1024, scale=None,
            silu=False, lane_blocked=False):
    m, k = a.shape
    n = w.shape[-2 if transposed else -1] if n is None else n
    tm = min(tm, m)
    tn = min(tn, n)
    assert m % tm == 0 and n % tn == 0 and col0 % tn == 0
    jb = col0 // tn
    w_block = (tn, k) if transposed else (k, tn)
    w_index = (lambda j: (j + jb, 0)) if transposed else (lambda j: (0, j + jb))
    if layer is None:
        w_spec = pl.BlockSpec(w_block, lambda j, i: w_index(j))
    else:
        w_spec = pl.BlockSpec((None,) + w_block, lambda j, i: (layer,) + w_index(j))
    if lane_blocked:
        out_spec = pl.BlockSpec((tn // LANES, tm, LANES), lambda j, i: (j, i, 0))
        out_shape = jax.ShapeDtypeStruct((n // LANES, m, LANES), out_dtype)
    else:
        out_spec = pl.BlockSpec((tm, tn), lambda j, i: (i, j))
        out_shape = jax.ShapeDtypeStruct((m, n), out_dtype)
    return pl.pallas_call(
        functools.partial(_mm_kernel, scale=scale, silu=silu, lane_blocked=lane_blocked, transposed=transposed),
        grid=(n // tn, m // tm),
        in_specs=[pl.BlockSpec((tm, k), lambda j, i: (i, 0)), w_spec],
        out_specs=out_spec,
        out_shape=out_shape,
        scratch_shapes=[] if w.dtype == BF16 else [pltpu.VMEM(w_block, BF16)],
        compiler_params=_params("parallel", "arbitrary"),
        name="matmul",
    )(a, w)


def _s5_scan_kernel(u_ref, wb_ref, cb_ref, a_ref, o_ref, bu_ref, hb_ref, carry_ref, *, t_tok):
    s_len = t_tok // SUBLANES
    p = SSM_CHUNK_STATE

    @pl.when(pl.program_id(2) == 0)
    def _():
        carry_ref[...] = jnp.zeros_like(carry_ref)
        hb_ref[...] = jnp.zeros_like(hb_ref)

    n_lb = SSM_CHUNK_IN // LANES
    rows = [jnp.concatenate([u_ref[c, pl.ds(k, SUBLANES, stride=s_len), :] for c in range(n_lb)], axis=1)
            for k in range(s_len)]
    up = jnp.concatenate(rows, axis=0).astype(BF16)
    bu_ref[...] = jnp.dot(up, wb_ref[...], preferred_element_type=F32)

    y = lax.dot_general(hb_ref[...].astype(BF16), cb_ref[...], (((1,), (1,)), ((), ())),
                        preferred_element_type=F32)
    for k in range(s_len):
        for c in range(n_lb):
            o_ref[c, pl.ds(k, SUBLANES, stride=s_len), :] = y[k * SUBLANES:(k + 1) * SUBLANES,
                                                               c * LANES:(c + 1) * LANES]

    sub = lax.broadcasted_iota(jnp.int32, (SUBLANES, LANES), 0)
    for blk in range(p // LANES):
        re = slice(2 * blk * LANES, (2 * blk + 1) * LANES)
        im = slice((2 * blk + 1) * LANES, (2 * blk + 2) * LANES)
        st = slice(blk * LANES, (blk + 1) * LANES)
        ar = jnp.broadcast_to(a_ref[0:1, st], (SUBLANES, LANES))
        ai = jnp.broadcast_to(a_ref[1:2, st], (SUBLANES, LANES))
        sr = jnp.broadcast_to(a_ref[2:3, st], (SUBLANES, LANES))
        si = jnp.broadcast_to(a_ref[3:4, st], (SUBLANES, LANES))

        def step(k, hr, hi):
            br = bu_ref[k * SUBLANES:(k + 1) * SUBLANES, re]
            bi = bu_ref[k * SUBLANES:(k + 1) * SUBLANES, im]
            return ar * hr - ai * hi + br, ar * hi + ai * hr + bi

        er = jnp.zeros((SUBLANES, LANES), F32)
        ei = jnp.zeros((SUBLANES, LANES), F32)
        for k in range(s_len):
            er, ei = step(k, er, ei)

        hinr = jnp.where(sub == 0, carry_ref[:, re], 0.0)
        hini = jnp.where(sub == 0, carry_ref[:, im], 0.0)
        for j in range(SUBLANES - 1):
            nr = sr * hinr - si * hini + er
            ni = sr * hini + si * hinr + ei
            hinr = jnp.where(sub == j + 1, pltpu.roll(nr, 1, 0), hinr)
            hini = jnp.where(sub == j + 1, pltpu.roll(ni, 1, 0), hini)
        nr = sr * hinr - si * hini + er
        ni = sr * hini + si * hinr + ei
        carry_ref[:, re] = jnp.broadcast_to(nr[SUBLANES - 1:SUBLANES, :], (SUBLANES, LANES))
        carry_ref[:, im] = jnp.broadcast_to(ni[SUBLANES - 1:SUBLANES, :], (SUBLANES, LANES))

        hr, hi = hinr, hini
        for k in range(s_len):
            hr, hi = step(k, hr, hi)
            hb_ref[k * SUBLANES:(k + 1) * SUBLANES, re] = hr
            hb_ref[k * SUBLANES:(k + 1) * SUBLANES, im] = hi


def _s5_scan(uz, wb, cb, acoef, bsz, seq):
    n_chunks = wb.shape[0]
    e = n_chunks * SSM_CHUNK_IN
    n_lb = SSM_CHUNK_IN // LANES
    t_tok = min(SCAN_T, seq)
    nt = seq // t_tok
    assert seq % t_tok == 0 and t_tok % (SUBLANES * SUBLANES) == 0
    return pl.pallas_call(
        functools.partial(_s5_scan_kernel, t_tok=t_tok),
        grid=(bsz, n_chunks, nt + 1),
        in_specs=[
            pl.BlockSpec((n_lb, t_tok, LANES), lambda b, c, t: (c, b * nt + jnp.minimum(t, nt - 1), 0)),
            pl.BlockSpec((None, SSM_CHUNK_IN, 2 * SSM_CHUNK_STATE), lambda b, c, t: (c, 0, 0)),
            pl.BlockSpec((None, SSM_CHUNK_IN, 2 * SSM_CHUNK_STATE), lambda b, c, t: (c, 0, 0)),
            pl.BlockSpec((None, SUBLANES, SSM_CHUNK_STATE), lambda b, c, t: (c, 0, 0)),
        ],
        out_specs=pl.BlockSpec((n_lb, t_tok, LANES), lambda b, c, t: (c, b * nt + jnp.maximum(t - 1, 0), 0)),
        out_shape=jax.ShapeDtypeStruct((e // LANES, bsz * seq, LANES), F32),
        scratch_shapes=[pltpu.VMEM((t_tok, 2 * SSM_CHUNK_STATE), F32),
                        pltpu.VMEM((t_tok, 2 * SSM_CHUNK_STATE), F32),
                        pltpu.VMEM((SUBLANES, 2 * SSM_CHUNK_STATE), F32)],
        compiler_params=_params("parallel", "parallel", "arbitrary"),
        name="s5_scan",
    )(uz, wb, cb, acoef)


def _s5_constants(lam_re, lam_im, log_dt, b_re, b_im, c_re, c_im, s_len):
    lr = lam_re.astype(F32)
    li = lam_im.astype(F32)
    dt = jnp.exp(log_dt.astype(F32))[:, None]
    mag = jnp.exp(lr * dt)
    ar = mag * jnp.cos(li * dt)
    ai = mag * jnp.sin(li * dt)
    inv_abs2 = 1.0 / (lr * lr + li * li)
    cr = ((ar - 1.0) * lr + ai * li) * inv_abs2
    ci = (ai * lr - (ar - 1.0) * li) * inv_abs2
    br = b_re.astype(F32)
    bi = b_im.astype(F32)
    bbar_r = cr[..., None] * br - ci[..., None] * bi
    bbar_i = cr[..., None] * bi + ci[..., None] * br
    g = lr.shape[0]
    nck = g // SSM_CHUNK_GROUPS
    n_cols = 2 * SSM_CHUNK_STATE
    col = jnp.arange(n_cols)
    col_group = 2 * (col // (2 * LANES)) + (col % LANES) // SSM_STATE
    on_diagonal = (jnp.arange(SSM_CHUNK_GROUPS)[:, None] == col_group[None, :])[None, :, None, :]

    def block_diagonal(w_re, w_im):
        pair = jnp.concatenate([w_re, w_re, w_im, w_im], axis=-1)
        row = jnp.tile(pair, (1, 1, n_cols // pair.shape[-1]))
        row = row.reshape(nck, SSM_CHUNK_GROUPS, SSM_GROUP, n_cols)
        return jnp.where(on_diagonal, row, 0.0).reshape(nck, SSM_CHUNK_IN, n_cols).astype(BF16)

    wb = block_diagonal(jnp.swapaxes(bbar_r, 1, 2), jnp.swapaxes(bbar_i, 1, 2))
    cbt = block_diagonal(c_re.astype(F32), -c_im.astype(F32))
    sr, si = ar, ai
    for _ in range(int(math.log2(s_len))):
        sr, si = sr * sr - si * si, 2.0 * sr * si
    rows = [v.reshape(nck, 1, SSM_CHUNK_STATE) for v in (ar, ai, sr, si)]
    rows += [jnp.zeros_like(rows[0])] * (SUBLANES - len(rows))
    return wb, cbt, jnp.concatenate(rows, axis=1)


def _lagged_out_proj(a_ref, x_ref, wo_ref, lg_ref, lb_ref, xo_ref, xb_ref, alpha):
    @pl.when(pl.program_id(0) == 0)
    def _():
        a_ref[...] = jnp.zeros_like(a_ref)

    o = jnp.dot(a_ref[...], wo_ref[...], preferred_element_type=F32)
    xn = _layer_norm(alpha * x_ref[...] + o, lg_ref[...], lb_ref[...])
    xo_ref[...] = xn
    xb_ref[...] = xn.astype(BF16)


def _s5_post_kernel(y_ref, u_ref, z_ref, x_ref, d_ref, wg_ref, bg_ref, wo_ref, lg_ref, lb_ref,
                    xo_ref, xb_ref, *, alpha):
    unblock = lambda ref: jnp.concatenate([ref[j] for j in range(ref.shape[0])], axis=1)
    y = unblock(y_ref) + d_ref[...] * unblock(u_ref)
    g = jax.nn.gelu(y)
    gl = jnp.dot(g.astype(BF16), wg_ref[...], preferred_element_type=F32) + bg_ref[...]
    y2 = g * _sigmoid(gl)
    z = unblock(z_ref)
    a = (y2 * (z * _sigmoid(z))).astype(BF16)
    o = jnp.dot(a, wo_ref[...], preferred_element_type=F32)
    xn = _layer_norm(alpha * x_ref[...] + o, lg_ref[...], lb_ref[...])
    xo_ref[...] = xn
    xb_ref[...] = xn.astype(BF16)


def _s5_post(yssm, uz, x, d, w_glu, b_glu, w_out, ln_g, ln_b, alpha, tm=256):
    nlb, m, _ = yssm.shape
    e = nlb * LANES
    dm = x.shape[1]
    tm = min(tm, m)
    row = lambda i: (i, 0)
    fixed = lambda i: (0, 0)
    return pl.pallas_call(
        functools.partial(_s5_post_kernel, alpha=alpha),
        grid=(m // tm,),
        in_specs=[pl.BlockSpec((nlb, tm, LANES), lambda i: (0, i, 0)),
                  pl.BlockSpec((nlb, tm, LANES), lambda i: (0, i, 0)),
                  pl.BlockSpec((nlb, tm, LANES), lambda i: (1, i, 0)),
                  pl.BlockSpec((tm, dm), row),
                  pl.BlockSpec((1, e), fixed),
                  pl.BlockSpec((e, e), fixed),
                  pl.BlockSpec((1, e), fixed),
                  pl.BlockSpec((e, dm), fixed),
                  pl.BlockSpec((1, dm), fixed),
                  pl.BlockSpec((1, dm), fixed)],
        out_specs=[pl.BlockSpec((tm, dm), row), pl.BlockSpec((tm, dm), row)],
        out_shape=[jax.ShapeDtypeStruct((m, dm), F32), jax.ShapeDtypeStruct((m, dm), BF16)],
        compiler_params=_params("parallel"),
        name="s5_post",
    )(yssm, uz, uz, x, d, w_glu, b_glu, w_out, ln_g, ln_b)


def _compress_kernel(t_ref, pos_ref, w1_ref, w2_ref, o_ref, q_ref, *, nc):
    half = CMP_BLOCK // 2
    acc_p = jnp.zeros((nc, HEAD_DIM), F32)
    acc_q = jnp.zeros((nc, HEAD_DIM), F32)
    for l in range(half):
        t = t_ref[pl.ds(l, nc, stride=CMP_STRIDE), :]
        lo = (t + pos_ref[l:l + 1, :]).astype(BF16)
        hi = (t + pos_ref[half + l:half + l + 1, :]).astype(BF16)
        acc_p += jnp.dot(lo, w1_ref[l * HEAD_DIM:(l + 1) * HEAD_DIM, :], preferred_element_type=F32)
        acc_q += jnp.dot(hi, w1_ref[(half + l) * HEAD_DIM:(half + l + 1) * HEAD_DIM, :],
                         preferred_element_type=F32)
    q_ref[0:nc, :] = acc_q
    q_ref[nc:nc + SUBLANES, :] = jnp.zeros((SUBLANES, HEAD_DIM), F32)
    pre = acc_p + q_ref[pl.ds(1, nc), :]
    mid = jax.nn.gelu(pre).astype(BF16)
    o_ref[...] = jnp.dot(mid, w2_ref[...], preferred_element_type=F32).astype(o_ref.dtype)


def _compress(kvc, pos, w1, w2, bsz, seq):
    nc = seq // CMP_STRIDE
    g = N_KV_HEADS
    return pl.pallas_call(
        functools.partial(_compress_kernel, nc=nc),
        grid=(bsz, 2, g),
        in_specs=[pl.BlockSpec((seq, HEAD_DIM), lambda b, s, h: (b, s * g + h)),
                  pl.BlockSpec((None, CMP_BLOCK, HEAD_DIM), lambda b, s, h: (s, 0, 0)),
                  pl.BlockSpec((None, CMP_BLOCK * HEAD_DIM, HEAD_DIM), lambda b, s, h: (s, 0, 0)),
                  pl.BlockSpec((None, HEAD_DIM, HEAD_DIM), lambda b, s, h: (s, 0, 0))],
        out_specs=pl.BlockSpec((None, None, None, nc, HEAD_DIM), lambda b, s, h: (b, s, h, 0, 0)),
        out_shape=jax.ShapeDtypeStruct((bsz, 2, g, nc, HEAD_DIM), BF16),
        scratch_shapes=[pltpu.VMEM((nc + SUBLANES, HEAD_DIM), F32)],
        compiler_params=_params("parallel", "parallel", "parallel"),
        name="kv_compress",
    )(kvc, pos, w1, w2)


def _cmp_select_kernel(q_ref, k_ref, v_ref, o_ref, sel_ref, *, tq, nc, ns, n_sel):
    hp = HEADS_PER_KV
    step = pl.program_id(2)
    q0 = step * tq
    q4 = jnp.concatenate([q_ref[:, h * HEAD_DIM:(h + 1) * HEAD_DIM] for h in range(hp)], axis=0)
    t = q0 + lax.broadcasted_iota(jnp.int32, (tq, nc), 0)
    n = lax.broadcasted_iota(jnp.int32, (tq, nc), 1)
    valid = (n * CMP_STRIDE + (CMP_BLOCK - 1) <= t)[None]
    s = lax.dot_general(q4, k_ref[...], (((1,), (1,)), ((), ())), preferred_element_type=F32)
    s = jnp.where(valid, s.reshape(hp, tq, nc), MASK_VALUE)
    m = jnp.max(s, axis=-1, keepdims=True)
    e = jnp.where(valid, jnp.exp(s - m), 0.0)
    p = e * (1.0 / jnp.maximum(jnp.sum(e, axis=-1, keepdims=True), 1e-30))
    o = jnp.dot(p.reshape(hp * tq, nc).astype(BF16), v_ref[...], preferred_element_type=F32)
    for h in range(hp):
        o_ref[:, h * HEAD_DIM:(h + 1) * HEAD_DIM] = o[h * tq:(h + 1) * tq, :].astype(o_ref.dtype)
    psum = p[0]
    for h in range(1, hp):
        psum = psum + p[h]

    nn = lax.broadcasted_iota(jnp.int32, (nc, LANES), 0)
    jj = lax.broadcasted_iota(jnp.int32, (nc, LANES), 1)
    overlap = ((nn * CMP_STRIDE < (jj + 1) * SEL_BLOCK)
               & (nn * CMP_STRIDE + (CMP_BLOCK - 1) >= jj * SEL_BLOCK)
               & (jj < ns)).astype(BF16)
    p_hi = psum.astype(BF16)
    p_lo = (psum - p_hi.astype(F32)).astype(BF16)
    imp = (jnp.dot(p_hi, overlap, preferred_element_type=F32)
           + jnp.dot(p_lo, overlap, preferred_element_type=F32))

    tt = q0 + lax.broadcasted_iota(jnp.int32, (tq, LANES), 0)
    j = lax.broadcasted_iota(jnp.int32, (tq, LANES), 1)
    cur = tt // SEL_BLOCK
    forced = (j == 0) | (j == cur) | (j == cur - 1)
    imp = jnp.where(j > cur, MASK_VALUE, jnp.where(forced, FORCE_SCORE, imp))
    imp = jnp.where(j < ns, imp, PAD_SCORE)

    x = imp.T
    sub = lax.broadcasted_iota(jnp.int32, (SUBLANES, tq), 0)

    def rank_and_store(live):
        nblk = live // SUBLANES
        if live <= n_sel:
            picked = [jnp.ones((SUBLANES, tq), F32)] * nblk
        else:
            xs = [x[b * SUBLANES:(b + 1) * SUBLANES, :] for b in range(nblk)]
            cnts = [jnp.zeros((SUBLANES, tq), jnp.int32) for _ in range(nblk)]
            for i in range(live):
                xi = x[i:i + 1, :]
                for b in range(nblk):
                    if b * SUBLANES > i:
                        beats = xi >= xs[b]
                    elif (b + 1) * SUBLANES - 1 <= i:
                        beats = xi > xs[b]
                    else:
                        beats = (xi > xs[b]) | ((xi >= xs[b]) & (sub > i - b * SUBLANES))
                    cnts[b] = cnts[b] + beats.astype(jnp.int32)
            picked = [(c < n_sel).astype(F32) for c in cnts]
        sel_t = jnp.concatenate(picked + [jnp.zeros((LANES - live, tq), F32)] * (live < LANES), axis=0)
        sel_ref[...] = sel_t.T.astype(sel_ref.dtype)

    for c in range(ns * SEL_BLOCK // tq):
        pl.when(step == c)(functools.partial(rank_and_store, (c + 1) * tq // SEL_BLOCK))


def _cmp_select(q, cmp_kv, bsz, seq):
    tq = min(CMP_TQ, seq)
    nq = seq // tq
    nc = seq // CMP_STRIDE
    ns = seq // SEL_BLOCK
    assert ns <= LANES and tq % LANES == 0 and tq % (SUBLANES * SEL_BLOCK) == 0
    g = N_KV_HEADS
    return pl.pallas_call(
        functools.partial(_cmp_select_kernel, tq=tq, nc=nc, ns=ns, n_sel=min(N_SELECT, ns)),
        grid=(bsz, g, nq),
        in_specs=[pl.BlockSpec((tq, GROUP_WIDTH), lambda b, h, i: (b * nq + i, h)),
                  pl.BlockSpec((None, None, None, nc, HEAD_DIM), lambda b, h, i: (b, 0, h, 0, 0)),
                  pl.BlockSpec((None, None, None, nc, HEAD_DIM), lambda b, h, i: (b, 1, h, 0, 0))],
        out_specs=[pl.BlockSpec((tq, GROUP_WIDTH), lambda b, h, i: (b * nq + i, h)),
                   pl.BlockSpec((None, None, tq, LANES), lambda b, h, i: (b, h, i, 0))],
        out_shape=[jax.ShapeDtypeStruct((bsz * seq, N_HEADS * HEAD_DIM), BF16),
                   jax.ShapeDtypeStruct((bsz, g, seq, LANES), BF16)],
        compiler_params=_params("parallel", "parallel", "parallel"),
        name="nsa_cmp_select",
    )(q, cmp_kv, cmp_kv)


def _lane_chunks(v):
    return [v[:, c * LANES:(c + 1) * LANES] for c in range(v.shape[1] // LANES)]


def _softmax_pv(chunks, v):
    mx = chunks[0]
    for sc in chunks[1:]:
        mx = jnp.maximum(mx, sc)
    m = jnp.broadcast_to(jnp.max(mx, axis=-1, keepdims=True), mx.shape)
    es = [jnp.exp2(sc - m) for sc in chunks]
    ls = es[0]
    for e in es[1:]:
        ls = ls + e
    e = jnp.concatenate([e.astype(BF16) for e in es], axis=1)
    return jnp.dot(e, v, preferred_element_type=F32) / jnp.sum(ls, axis=-1, keepdims=True)


def _slc_win_kernel(q_ref, sel_ref, oh_ref, ks_ref, vs_ref, kw_ref, vw_ref, os_ref, ow_ref, s_ref, *,
                    tq, tk, seq):
    hp = HEADS_PER_KV
    rows = hp * tq
    q0 = pl.program_id(2) * tq
    q4 = jnp.concatenate([q_ref[:, h * HEAD_DIM:(h + 1) * HEAD_DIM] for h in range(hp)], axis=0)
    neg = ((sel_ref[...].astype(F32) - 1.0) * (-MASK_VALUE)).astype(BF16)
    q_aug = jnp.concatenate([q4, jnp.concatenate([neg] * hp, axis=0)], axis=1)
    nt_dims = (((1,), (1,)), ((), ()))
    n_tiles = (q0 + tq - 1) // tk + 1
    last = n_tiles - 1

    def masked_scores(kt):
        k0 = pl.multiple_of(kt * tk, tk)
        k_aug = jnp.concatenate([ks_ref[pl.ds(k0, tk), :], oh_ref[pl.ds(k0, tk), :]], axis=1)
        s = lax.dot_general(q_aug, k_aug, nt_dims, preferred_element_type=F32)
        return s * LOG2_E

    def lane_max(s, mx):
        for sc in _lane_chunks(s):
            mx = jnp.maximum(mx, sc)
        return mx

    def scores(kt, mx):
        s = masked_scores(kt)
        s_ref[kt] = s
        return lane_max(s, mx)

    mx = lax.fori_loop(0, last, scores, jnp.full((rows, LANES), MASK_VALUE, F32))
    t = q0 + lax.broadcasted_iota(jnp.int32, (tq, tk), 0)
    kcol = last * tk + lax.broadcasted_iota(jnp.int32, (tq, tk), 1)
    s = jnp.where((kcol <= t)[None], masked_scores(last).reshape(hp, tq, tk), MASK_VALUE).reshape(rows, tk)
    s_ref[last] = s
    mx = lane_max(s, mx)
    m = jnp.broadcast_to(jnp.max(mx, axis=-1, keepdims=True), (rows, LANES))

    def weighted(kt, carry):
        ls, acc = carry
        k0 = pl.multiple_of(kt * tk, tk)
        es = [jnp.exp2(sc - m) for sc in _lane_chunks(s_ref[kt])]
        for e in es:
            ls = ls + e
        e = jnp.concatenate([e.astype(BF16) for e in es], axis=1)
        return ls, acc + jnp.dot(e, vs_ref[pl.ds(k0, tk), :], preferred_element_type=F32)

    ls, acc = lax.fori_loop(0, n_tiles, weighted,
                            (jnp.zeros((rows, LANES), F32), jnp.zeros((rows, HEAD_DIM), F32)))
    o = acc / jnp.sum(ls, axis=-1, keepdims=True)
    for h in range(hp):
        os_ref[:, h * HEAD_DIM:(h + 1) * HEAD_DIM] = o[h * tq:(h + 1) * tq, :].astype(os_ref.dtype)

    tw = min(WIN_TQ, tq)
    span = min(WINDOW + tw, seq)
    for j in range(tq // tw):
        qw0 = q0 + j * tw
        start = pl.multiple_of(jnp.clip(qw0 + tw - span, 0, seq - span), tw)
        kpos = start + lax.broadcasted_iota(jnp.int32, (tw, span), 1)
        tpos = qw0 + lax.broadcasted_iota(jnp.int32, (tw, span), 0)
        wbias = jnp.where((kpos <= tpos) & (kpos > tpos - WINDOW), 0.0, MASK_VALUE)
        qw = jnp.concatenate([q_ref[j * tw:(j + 1) * tw, h * HEAD_DIM:(h + 1) * HEAD_DIM] for h in range(hp)],
                             axis=0)
        sw = lax.dot_general(qw, kw_ref[pl.ds(start, span), :], nt_dims, preferred_element_type=F32) * LOG2_E
        sw = (sw.reshape(hp, tw, span) + wbias[None]).reshape(hp * tw, span)
        ow = _softmax_pv(_lane_chunks(sw), vw_ref[pl.ds(start, span), :])
        for h in range(hp):
            ow_ref[j * tw:(j + 1) * tw, h * HEAD_DIM:(h + 1) * HEAD_DIM] = ow[h * tw:(h + 1) * tw, :].astype(
                ow_ref.dtype)


def _slc_win(q, sel, kvs, bsz, seq):
    tq = min(ATT_TQ, seq)
    tk = min(ATT_TK, seq)
    nq = seq // tq
    g = N_KV_HEADS
    block_onehot = (jnp.arange(seq)[:, None] // SEL_BLOCK == jnp.arange(LANES)[None, :]).astype(BF16)
    kv_spec = lambda part: pl.BlockSpec((seq, HEAD_DIM), lambda b, h, i: (b, part * g + h))
    out_spec = pl.BlockSpec((tq, GROUP_WIDTH), lambda b, h, i: (b * nq + i, h))
    out_sds = jax.ShapeDtypeStruct((bsz * seq, N_HEADS * HEAD_DIM), BF16)
    return pl.pallas_call(
        functools.partial(_slc_win_kernel, tq=tq, tk=tk, seq=seq),
        grid=(bsz, g, nq),
        in_specs=[pl.BlockSpec((tq, GROUP_WIDTH), lambda b, h, i: (b * nq + i, h)),
                  pl.BlockSpec((None, None, tq, LANES), lambda b, h, i: (b, h, i, 0)),
                  pl.BlockSpec((seq, LANES), lambda b, h, i: (0, 0)),
                  kv_spec(0), kv_spec(1), kv_spec(2), kv_spec(3)],
        out_specs=[out_spec, out_spec],
        out_shape=[out_sds, out_sds],
        scratch_shapes=[pltpu.VMEM((seq // tk, HEADS_PER_KV * tq, tk), F32)],
        compiler_params=pltpu.CompilerParams(dimension_semantics=("parallel", "parallel", "parallel"),
                                             vmem_limit_bytes=ATT_VMEM_LIMIT_BYTES),
        name="nsa_slc_win",
    )(q, sel, block_onehot, kvs, kvs, kvs, kvs)


def _nsa_post_kernel(oc_ref, os_ref, ow_ref, sz_ref, gl_ref, x_ref, wo_ref, lg_ref, lb_ref,
                     xo_ref, xb_ref, y_ref, *, alpha):
    _lagged_out_proj(y_ref, x_ref, wo_ref, lg_ref, lb_ref, xo_ref, xb_ref, alpha)
    width = N_HEADS * HEAD_DIM
    gate = _sigmoid(gl_ref[...])
    for h in range(N_HEADS):
        hs = slice(h * HEAD_DIM, (h + 1) * HEAD_DIM)
        acc = None
        for br, o_ref in enumerate((oc_ref, os_ref, ow_ref)):
            gcol = gate[:, br * N_HEADS + h:br * N_HEADS + h + 1]
            sz = sz_ref[:, br * width + h * HEAD_DIM:br * width + (h + 1) * HEAD_DIM]
            term = gcol * o_ref[:, hs].astype(F32) * sz.astype(F32)
            acc = term if acc is None else acc + term
        y_ref[:, hs] = acc.astype(BF16)


def _nsa_post(o_cmp, o_slc, o_win, sz, gate_logits, x, w_out, ln_g, ln_b, alpha, tm=256):
    m, dm = x.shape
    width = N_HEADS * HEAD_DIM
    tm = min(tm, m)
    nt = m // tm
    cur = lambda i: (jnp.minimum(i, nt - 1), 0)
    prev = lambda i: (jnp.maximum(i - 1, 0), 0)
    fixed = lambda i: (0, 0)
    return pl.pallas_call(
        functools.partial(_nsa_post_kernel, alpha=alpha),
        grid=(nt + 1,),
        in_specs=[pl.BlockSpec((tm, width), cur),
                  pl.BlockSpec((tm, width), cur),
                  pl.BlockSpec((tm, width), cur),
                  pl.BlockSpec((tm, N_BRANCH * width), cur),
                  pl.BlockSpec((tm, LANES), cur),
                  pl.BlockSpec((tm, dm), prev),
                  pl.BlockSpec((width, dm), fixed, pipeline_mode=pl.Buffered(1)),
                  pl.BlockSpec((1, dm), fixed),
                  pl.BlockSpec((1, dm), fixed)],
        out_specs=[pl.BlockSpec((tm, dm), prev), pl.BlockSpec((tm, dm), prev)],
        out_shape=[jax.ShapeDtypeStruct((m, dm), F32), jax.ShapeDtypeStruct((m, dm), BF16)],
        scratch_shapes=[pltpu.VMEM((tm, width), BF16)],
        compiler_params=_params("arbitrary"),
        name="nsa_post",
    )(o_cmp, o_slc, o_win, sz, gate_logits, x, w_out, ln_g, ln_b)


def kernel(x, a_w_in, a_lam_re, a_lam_im, a_log_dt, a_b_re, a_b_im, a_c_re, a_c_im, a_d, a_w_glu, a_b_glu, a_w_out, kv_w, cmp_pos_k, cmp_w1_k, cmp_w2_k, cmp_pos_v, cmp_w1_v, cmp_w2_v, b_w_in, b_w_out, ln_g, ln_b):
    bsz, seq, dm = x.shape
    n_a = a_w_in.shape[0]
    n_b = b_w_in.shape[0]
    depth = n_a + n_b
    alpha = (2 * depth) ** 0.25
    width = N_HEADS * HEAD_DIM
    kvw = N_KV_HEADS * HEAD_DIM
    s_len = min(SCAN_T, seq) // SUBLANES

    xf = x.reshape(bsz * seq, dm).astype(F32)
    xb = xf

    for i in range(n_a):
        wb, cb, acoef = _s5_constants(a_lam_re[i], a_lam_im[i], a_log_dt[i], a_b_re[i], a_b_im[i],
                                      a_c_re[i], a_c_im[i], s_len)
        e = a_w_glu.shape[1]
        uz = _matmul(xb, a_w_in.astype(F32), F32, layer=i, lane_blocked=True)
        yssm = _s5_scan(uz, wb, cb, acoef, bsz, seq)
        xf, xb = _s5_post(yssm, uz, xf, a_d[i].reshape(1, e).astype(F32), a_w_glu[i].astype(BF16),
                          a_b_glu[i].reshape(1, e).astype(F32), a_w_out[i].astype(BF16),
                          ln_g[i].reshape(1, dm).astype(F32), ln_b[i].reshape(1, dm).astype(F32), alpha)

    kvc = _matmul(xb, kv_w.astype(F32), F32, n=2 * kvw)
    kvs = _matmul(xb, kv_w.astype(F32), BF16, col0=2 * kvw, n=4 * kvw)
    cmp_kv = _compress(kvc,
                       jnp.stack([cmp_pos_k, cmp_pos_v]).astype(F32),
                       jnp.stack([cmp_w1_k, cmp_w1_v]).astype(BF16),
                       jnp.stack([cmp_w2_k, cmp_w2_v]).astype(BF16), bsz, seq)

    b_w_in_t = jnp.swapaxes(b_w_in, 1, 2).astype(F32)
    n_main = (1 + N_BRANCH) * width
    for i in range(n_b):
        layer = n_a + i
        w_gate_t = b_w_in_t[i, n_main:, :]
        w_gate_t = jnp.pad(w_gate_t, ((0, LANES - w_gate_t.shape[0]), (0, 0)))
        q = _matmul(xb, b_w_in_t, BF16, layer=i, transposed=True, n=width, scale=HEAD_DIM ** -0.5)
        sz = _matmul(xb, b_w_in_t, BF16, layer=i, transposed=True, col0=width, n=N_BRANCH * width, silu=True)
        gate_logits = _matmul(xb, w_gate_t, F32, transposed=True)
        o_cmp, sel = _cmp_select(q, cmp_kv, bsz, seq)
        o_slc, o_win = _slc_win(q, sel, kvs, bsz, seq)
        xf, xb = _nsa_post(o_cmp, o_slc, o_win, sz, gate_logits, xf, b_w_out[i].astype(BF16),
                           ln_g[layer].reshape(1, dm).astype(F32), ln_b[layer].reshape(1, dm).astype(F32), alpha)

    return xf.reshape(bsz, seq, dm).astype(x.dtype)
```
